```python
import math
import jax, jax.numpy as jnp
from jax import lax
import numpy as np

D_MODEL = 2048
BATCH = 2
SEQ = 16384
DEPTH = 2

N_MIXERS = 2
N_CONV_LAYERS = (DEPTH + 1) // 2
N_ATTN_LAYERS = DEPTH // 2
CONV_WIDTH = 3
N_HEADS = 8
HEAD_DIM = D_MODEL // N_HEADS
Q_BLOCK = 128
RMS_EPS = 1e-6

kernel_name = "hybrid_shortconv_stickbreaking_adaln"


def rmsnorm(x, g):
    xf = x.astype(jnp.float32)
    y = xf * lax.rsqrt(jnp.mean(xf * xf, axis=-1, keepdims=True) + RMS_EPS)
    return (y * g.astype(jnp.float32)).astype(x.dtype)


def short_conv_mixer(h, w_in, conv_w, conv_b, w_out):
    S = h.shape[1]
    b_gate, c_gate, u, g = jnp.split(h @ w_in, 4, axis=-1)
    v = c_gate * u
    vp = jnp.pad(v, ((0, 0), (CONV_WIDTH - 1, 0), (0, 0)))
    conv = conv_b + sum(vp[:, k:k + S] * conv_w[k] for k in range(CONV_WIDTH))
    y = b_gate * conv * jax.nn.silu(g)
    return y @ w_out


def stick_breaking_block(q_blk, t0, k, v):
    T1 = k.shape[2]
    z = jnp.einsum('bhqd,bhkd->bhqk', q_blk, k).astype(jnp.float32) * (1.0 / math.sqrt(HEAD_DIM))
    t_idx = t0 + jnp.arange(Q_BLOCK)[:, None]
    s_idx = jnp.arange(T1)[None, :]
    past = s_idx < t_idx
    log_not = jnp.where(past, jax.nn.log_sigmoid(-z), 0.0)
    between = lax.cumsum(log_not, axis=3, reverse=True) - log_not
    a = jnp.where(past, jnp.exp(jax.nn.log_sigmoid(z) + between), 0.0)
    return jnp.einsum('bhqk,bhkd->bhqd', a.astype(v.dtype), v)


def stick_breaking_mixer(h, w_in, w_out):
    Bsz, S, D = h.shape
    q, k, v, g = jnp.split(h @ w_in, 4, axis=-1)
    to_heads = lambda t: t.reshape(Bsz, S, N_HEADS, HEAD_DIM).transpose(0, 2, 1, 3)
    q, k, v = to_heads(q), to_heads(k), to_heads(v)
    outs = []
    for i in range(S // Q_BLOCK):
        t0 = i * Q_BLOCK
        t1 = t0 + Q_BLOCK
        outs.append(stick_breaking_block(q[:, :, t0:t1], t0, k[:, :, :t1], v[:, :, :t1]))
    out = jnp.concatenate(outs, axis=2)
    out = out.transpose(0, 2, 1, 3).reshape(Bsz, S, D)
    return (out * jax.nn.silu(g)) @ w_out


def setup_inputs(seed: int = 0) -> dict:
    key = jax.random.key(seed)
    ks = jax.random.split(key, 14)
    D = D_MODEL
    f32 = jnp.float32
    s_in = D ** -0.5
    return {
        "x": jax.random.normal(ks[0], (BATCH, SEQ, D), f32),
        "c": jax.random.normal(ks[1], (BATCH, D), f32),
        "norm_g": 1.0 + 0.02 * jax.random.normal(ks[2], (DEPTH, D), f32),
        "ada_w": 0.5 * s_in * jax.random.normal(ks[3], (DEPTH, D, 3 * D), f32),
        "ada_b": 0.02 * jax.random.normal(ks[4], (DEPTH, 3 * D), f32),
        "conv_in_w": s_in * jax.random.normal(ks[5], (N_CONV_LAYERS, D, 4 * D), f32),
        "conv_w": CONV_WIDTH ** -0.5 * jax.random.normal(ks[6], (N_CONV_LAYERS, CONV_WIDTH, D), f32),
        "conv_b": 0.02 * jax.random.normal(ks[7], (N_CONV_LAYERS, D), f32),
        "conv_out_w": s_in * jax.random.normal(ks[8], (N_CONV_LAYERS, D, D), f32),
        "attn_in_w": s_in * jax.random.normal(ks[9], (N_ATTN_LAYERS, D, 4 * D), f32),
        "attn_out_w": s_in * jax.random.normal(ks[10], (N_ATTN_LAYERS, D, D), f32),
        "final_g": 1.0 + 0.02 * jax.random.normal(ks[11], (D,), f32),
    }


def reference(x, c, norm_g, ada_w, ada_b, conv_in_w, conv_w, conv_b, conv_out_w,
              attn_in_w, attn_out_w, final_g):
    c_act = jax.nn.silu(c)
    for i in range(DEPTH):
        mod = c_act @ ada_w[i] + ada_b[i]
        shift, scale, gate = jnp.split(mod, 3, axis=-1)
        h = rmsnorm(x, norm_g[i]) * (1.0 + scale[:, None, :]) + shift[:, None, :]
        j = i // N_MIXERS
        if i % N_MIXERS == 0:
            y = short_conv_mixer(h, conv_in_w[j], conv_w[j], conv_b[j], conv_out_w[j])
        else:
            y = stick_breaking_mixer(h, attn_in_w[j], attn_out_w[j])
        x = x + gate[:, None, :] * y
    return rmsnorm(x, final_g)
```

```python
import functools
import math

import jax
import jax.numpy as jnp
from jax import lax
from jax.experimental import pallas as pl
from jax.experimental.pallas import tpu as pltpu

N_HEADS = 8
CONV_WIDTH = 3
RMS_EPS = 1e-6

LANES = 128
SUBLANES = 8
MXU_DIM = 256
MIB = 1024 * 1024

F32 = jnp.float32
BF16 = jnp.bfloat16


def _silu(x):
    return x / (1.0 + jnp.exp(-x))


def _dot(a, b):
    return jnp.dot(a, b, preferred_element_type=F32)


def _norm_modulate(x, norm_g, shift, scale):
    ms = jnp.mean(x * x, axis=-1, keepdims=True)
    return (x * lax.rsqrt(ms + RMS_EPS) * norm_g) * (1.0 + scale) + shift


def _adaln_kernel(c_ref, w_ref, b_ref, o_ref):
    c_act = _silu(c_ref[...])
    o_ref[0] = jnp.dot(c_act, w_ref[0], preferred_element_type=F32,
                       precision=lax.Precision.HIGHEST) + b_ref[0]


def _adaln_mod(c, ada_w, ada_b):
    depth, d, n3 = ada_w.shape
    bsz = c.shape[0]
    tn = 768
    return pl.pallas_call(
        _adaln_kernel,
        grid=(depth, n3 // tn),
        in_specs=[
            pl.BlockSpec((bsz, d), lambda l, n: (0, 0)),
            pl.BlockSpec((1, d, tn), lambda l, n: (l, 0, n)),
            pl.BlockSpec((1, 1, tn), lambda l, n: (l, 0, n)),
        ],
        out_specs=pl.BlockSpec((1, bsz, tn), lambda l, n: (l, 0, n)),
        out_shape=jax.ShapeDtypeStruct((depth, bsz, n3), F32),
        compiler_params=pltpu.CompilerParams(
            dimension_semantics=("arbitrary", "arbitrary"),
            vmem_limit_bytes=32 * MIB),
        name="adaln_mod",
    )(c, ada_w, ada_b.reshape(depth, 1, n3))


def _conv_in_kernel(x_ref, mod_ref, ng_ref, wb_ref, wc_ref, wu_ref, wg_ref, cw_ref, cb_ref,
                    y_ref, h_ref, halo_ref, *, tiles_per_batch):
    i = pl.program_id(0)
    j = pl.program_id(1)
    tm = x_ref.shape[0]

    @pl.when(j == 0)
    def _():
        h = _norm_modulate(x_ref[...], ng_ref[...], mod_ref[0, 0:1, :], mod_ref[0, 1:2, :])
        h_ref[...] = h.astype(BF16)

    h = h_ref[...]
    v = _dot(h, wc_ref[...]) * _dot(h, wu_ref[...])
    first = (i % tiles_per_batch) == 0
    halo = jnp.where(first, 0.0, halo_ref[j])
    halo_ref[j] = v[tm - SUBLANES:, :]
    vfull = jnp.concatenate([halo, v], axis=0)
    vm1 = pltpu.roll(vfull, 1, 0)[SUBLANES:, :]
    vm2 = pltpu.roll(vfull, 2, 0)[SUBLANES:, :]
    conv = cb_ref[...] + cw_ref[0:1, :] * vm2 + cw_ref[1:2, :] * vm1 + cw_ref[2:3, :] * v
    y = _dot(h, wb_ref[...]) * conv * _silu(_dot(h, wg_ref[...]))
    y_ref[...] = y.astype(BF16)


def _conv_in(x2, mod, norm_g, w_in, conv_w, conv_b, *, seq, tm, cn):
    rows, d = x2.shape
    nb = d // cn
    tiles_per_batch = seq // tm
    kern = functools.partial(_conv_in_kernel, tiles_per_batch=tiles_per_batch)
    w_spec = lambda part: pl.BlockSpec((d, cn), lambda i, j, part=part: (0, part * nb + j))
    return pl.pallas_call(
        kern,
        grid=(rows // tm, nb),
        in_specs=[
            pl.BlockSpec((tm, d), lambda i, j: (i, 0)),
            pl.BlockSpec((1, 3, d), lambda i, j: (i // tiles_per_batch, 0, 0)),
            pl.BlockSpec((1, d), lambda i, j: (0, 0)),
            w_spec(0), w_spec(1), w_spec(2), w_spec(3),
            pl.BlockSpec((CONV_WIDTH, cn), lambda i, j: (0, j)),
            pl.BlockSpec((1, cn), lambda i, j: (0, j)),
        ],
        out_specs=pl.BlockSpec((tm, cn), lambda i, j: (i, j)),
        out_shape=jax.ShapeDtypeStruct((rows, d), BF16),
        scratch_shapes=[
            pltpu.VMEM((tm, d), BF16),
            pltpu.VMEM((nb, SUBLANES, cn), F32),
        ],
        compiler_params=pltpu.CompilerParams(
            dimension_semantics=("arbitrary", "arbitrary"),
            vmem_limit_bytes=48 * MIB),
        name="conv_in",
    )(x2, mod, norm_g, w_in, w_in, w_in, w_in, conv_w, conv_b)


def _out_residual_kernel(y_ref, x_ref, mod_ref, w_ref, fg_ref, o_ref, *, final_norm):
    r = x_ref[...] + mod_ref[0, 2:3, :] * _dot(y_ref[...], w_ref[...])
    if final_norm:
        ms = jnp.mean(r * r, axis=-1, keepdims=True)
        r = r * lax.rsqrt(ms + RMS_EPS) * fg_ref[...]
    o_ref[...] = r


def _out_residual(y2, x2, mod, w, final_g, *, seq, tm, final_norm, name):
    rows, d = x2.shape
    tiles_per_batch = seq // tm
    kern = functools.partial(_out_residual_kernel, final_norm=final_norm)
    return pl.pallas_call(
        kern,
        grid=(rows // tm,),
        in_specs=[
            pl.BlockSpec((tm, d), lambda i: (i, 0)),
            pl.BlockSpec((tm, d), lambda i: (i, 0)),
            pl.BlockSpec((1, 3, d), lambda i: (i // tiles_per_batch, 0, 0)),
            pl.BlockSpec((d, d), lambda i: (0, 0)),
            pl.BlockSpec((1, d), lambda i: (0, 0)),
        ],
        out_specs=pl.BlockSpec((tm, d), lambda i: (i, 0)),
        out_shape=jax.ShapeDtypeStruct((rows, d), F32),
        compiler_params=pltpu.CompilerParams(
            dimension_semantics=("arbitrary",),
            vmem_limit_bytes=48 * MIB),
        name=name,
    )(y2, x2, mod, w, final_g)


def _attn_in_kernel(x_ref, mod_ref, ng_ref, w_ref, o_ref, h_ref, *, q_blocks, q_scale):
    n = pl.program_id(1)

    @pl.when(n == 0)
    def _():
        h = _norm_modulate(x_ref[...], ng_ref[...], mod_ref[0, 0:1, :], mod_ref[0, 1:2, :])
        h_ref[...] = h.astype(BF16)

    p = _dot(h_ref[...], w_ref[...])
    p = p * jnp.where(n < q_blocks, q_scale, 1.0)
    dh = o_ref.shape[3]
    for hh in range(o_ref.shape[1]):
        o_ref[0, hh] = p[:, hh * dh:(hh + 1) * dh].astype(BF16)


def _attn_in(x2, mod, norm_g, w_in, *, bsz, seq, dh, tm, tn):
    rows, d = x2.shape
    n4 = w_in.shape[1]
    tiles_per_batch = seq // tm
    heads_per_block = tn // dh
    kern = functools.partial(_attn_in_kernel, q_blocks=d // tn, q_scale=1.0 / math.sqrt(dh))
    return pl.pallas_call(
        kern,
        grid=(rows // tm, n4 // tn),
        in_specs=[
            pl.BlockSpec((tm, d), lambda i, n: (i, 0)),
            pl.BlockSpec((1, 3, d), lambda i, n: (i // tiles_per_batch, 0, 0)),
            pl.BlockSpec((1, d), lambda i, n: (0, 0)),
            pl.BlockSpec((d, tn), lambda i, n: (0, n)),
        ],
        out_specs=pl.BlockSpec(
            (1, heads_per_block, tm, dh),
            lambda i, n: (i // tiles_per_batch, n, i % tiles_per_batch, 0)),
        out_shape=jax.ShapeDtypeStruct((bsz, n4 // dh, seq, dh), BF16),
        scratch_shapes=[pltpu.VMEM((tm, d), BF16)],
        compiler_params=pltpu.CompilerParams(
            dimension_semantics=("arbitrary", "arbitrary"),
            vmem_limit_bytes=48 * MIB),
        name="attn_in",
    )(x2, mod, norm_g, w_in)


def _stick_attn_kernel(q_ref, k_ref, v_ref, g_ref, u_ref, o_ref, acc_ref, run_ref, *, tk):
    qi = pl.program_id(2)
    tq = q_ref.shape[2]
    diag_blocks = tq // tk
    q = q_ref[0, 0]
    q0 = qi * tq
    acc_ref[...] = jnp.zeros_like(acc_ref)
    run_ref[...] = jnp.zeros_like(run_ref)
    t_idx = q0 + lax.broadcasted_iota(jnp.int32, (tq, tk), 0)
    s_off = lax.broadcasted_iota(jnp.int32, (tq, tk), 1)

    def key_block(kb, masked):
        k0 = pl.multiple_of(kb * tk, tk)
        z = lax.dot_general(q, k_ref[0, 0, pl.ds(k0, tk), :], (((1,), (1,)), ((), ())),
                            preferred_element_type=F32)
        sp = jnp.maximum(z, 0.0) + jnp.log(1.0 + jnp.exp(-jnp.abs(z)))
        log_not = -sp
        log_beta = z - sp
        if masked:
            past = (k0 + s_off) < t_idx
            log_not = jnp.where(past, log_not, 0.0)
        hi = log_not.astype(BF16)
        lo = (log_not - hi.astype(F32)).astype(BF16)
        cum = _dot(jnp.concatenate([hi, lo], axis=1), u_ref[...])
        run = run_ref[...]
        a = jnp.exp(log_beta + cum + run)
        if masked:
            a = jnp.where(past, a, 0.0)
        acc_ref[...] += _dot(a.astype(BF16), v_ref[0, 0, pl.ds(k0, tk), :])
        run_ref[...] = run + cum[:, 0:1] + log_not[:, 0:1]

    for dblk in reversed(range(diag_blocks)):
        key_block(qi * diag_blocks + dblk, masked=True)

    def body(n, carry):
        key_block(qi * diag_blocks - 1 - n, masked=False)
        return carry

    lax.fori_loop(0, qi * diag_blocks, body, 0)
    o_ref[0] = (acc_ref[...] * _silu(g_ref[0, 0].astype(F32))).astype(BF16)


def _stick_attn(qkvg, *, n_heads, tq, tk):
    bsz, _, seq, dh = qkvg.shape
    tri = (lax.broadcasted_iota(jnp.int32, (tk, tk), 0)
           > lax.broadcasted_iota(jnp.int32, (tk, tk), 1)).astype(BF16)
    u2 = jnp.concatenate([tri, tri], axis=0)
    kern = functools.partial(_stick_attn_kernel, tk=tk)
    return pl.pallas_call(
        kern,
        grid=(bsz, n_heads, seq // tq),
        in_specs=[
            pl.BlockSpec((1, 1, tq, dh), lambda b, h, qi: (b, h, qi, 0)),
            pl.BlockSpec((1, 1, seq, dh), lambda b, h, qi: (b, n_heads + h, 0, 0)),
            pl.BlockSpec((1, 1, seq, dh), lambda b, h, qi: (b, 2 * n_heads + h, 0, 0)),
            pl.BlockSpec((1, 1, tq, dh), lambda b, h, qi: (b, 3 * n_heads + h, qi, 0)),
            pl.BlockSpec((2 * tk, tk), lambda b, h, qi: (0, 0)),
        ],
        out_specs=pl.BlockSpec((1, tq, dh), lambda b, h, qi: (b, qi, h)),
        out_shape=jax.ShapeDtypeStruct((bsz, seq, n_heads * dh), BF16),
        scratch_shapes=[
            pltpu.VMEM((tq, dh), F32),
            pltpu.VMEM((tq, 1), F32),
        ],
        compiler_params=pltpu.CompilerParams(
            dimension_semantics=("arbitrary", "arbitrary", "arbitrary"),
            vmem_limit_bytes=48 * MIB),
        name="stick_attn",
    )(qkvg, qkvg, qkvg, qkvg, u2)


def kernel(x, c, norm_g, ada_w, ada_b, conv_in_w, conv_w, conv_b, conv_out_w, attn_in_w, attn_out_w, final_g):
    bsz, seq, d = x.shape
    dh = d // N_HEADS
    tm = min(512, seq)
    x2 = x.reshape(bsz * seq, d)
    mod = _adaln_mod(c, ada_w, ada_b)
    mod0 = mod[0].reshape(bsz, 3, d)
    mod1 = mod[1].reshape(bsz, 3, d)
    fg = final_g.reshape(1, d)

    y0 = _conv_in(x2, mod0, norm_g[0:1], conv_in_w[0].astype(BF16), conv_w[0], conv_b[0:1],
                  seq=seq, tm=tm, cn=MXU_DIM)
    x1 = _out_residual(y0, x2, mod0, conv_out_w[0].astype(BF16), fg,
                       seq=seq, tm=tm, final_norm=False, name="conv_out")
    qkvg = _attn_in(x1, mod1, norm_g[1:2], attn_in_w[0].astype(BF16),
                    bsz=bsz, seq=seq, dh=dh, tm=tm, tn=1024)
    o = _stick_attn(qkvg, n_heads=N_HEADS, tq=MXU_DIM, tk=MXU_DIM)
    out = _out_residual(o.reshape(bsz * seq, d), x1, mod1, attn_out_w[0].astype(BF16), fg,
                        seq=seq, tm=tm, final_norm=True, name="attn_out")
    return out.reshape(bsz, seq, d)
```

```python
import functools
import math

import jax
import jax.numpy as jnp
from jax import lax
from jax.experimental import pallas as pl
from jax.experimental.pallas import tpu as pltpu

N_HEADS = 8
CONV_WIDTH = 3
RMS_EPS = 1e-6

SUBLANES = 8
MXU_DIM = 256
MIB = 1024 * 1024
VMEM_LIMIT = 56 * MIB

EXP2_ZERO_ABOVE = 151.0
LOG2E = 1.4426950408889634

F32 = jnp.float32
BF16 = jnp.bfloat16


def _silu(x):
    return x / (1.0 + jnp.exp(-x))


def _dot(a, b):
    return jnp.dot(a, b, preferred_element_type=F32)


def _norm_modulate(x, norm_g, shift, scale):
    ms = jnp.mean(x * x, axis=-1, keepdims=True)
    return (x * lax.rsqrt(ms + RMS_EPS) * norm_g) * (1.0 + scale) + shift


def _params(*semantics):
    return pltpu.CompilerParams(dimension_semantics=semantics, vmem_limit_bytes=VMEM_LIMIT)


def _adaln_kernel(c_ref, w_ref, b_ref, o_ref):
    c_act = _silu(c_ref[...])
    o_ref[0] = jnp.dot(c_act, w_ref[0], preferred_element_type=F32,
                       precision=lax.Precision.HIGHEST) + b_ref[0]


def _adaln_mod(c, ada_w, ada_b):
    depth, d, n3 = ada_w.shape
    bsz = c.shape[0]
    tn = 768
    return pl.pallas_call(
        _adaln_kernel,
        grid=(depth, n3 // tn),
        in_specs=[
            pl.BlockSpec((bsz, d), lambda l, n: (0, 0)),
            pl.BlockSpec((1, d, tn), lambda l, n: (l, 0, n)),
            pl.BlockSpec((1, 1, tn), lambda l, n: (l, 0, n)),
        ],
        out_specs=pl.BlockSpec((1, bsz, tn), lambda l, n: (l, 0, n)),
        out_shape=jax.ShapeDtypeStruct((depth, bsz, n3), F32),
        compiler_params=_params("arbitrary", "arbitrary"),
        name="adaln_mod",
    )(c, ada_w, ada_b.reshape(depth, 1, n3))


def _conv_in_kernel(x_ref, mod_ref, ng_ref, wb_ref, wc_ref, wu_ref, wg_ref, cw_ref, cb_ref,
                    y_ref, h_ref, halo_ref, *, tiles_per_batch):
    i = pl.program_id(0)
    j = pl.program_id(1)
    tm = x_ref.shape[0]

    @pl.when(j == 0)
    def _():
        h = _norm_modulate(x_ref[...], ng_ref[...], mod_ref[0, 0:1, :], mod_ref[0, 1:2, :])
        h_ref[...] = h.astype(BF16)

    h = h_ref[...]
    v = _dot(h, wc_ref[...]) * _dot(h, wu_ref[...])
    first = (i % tiles_per_batch) == 0
    halo = jnp.where(first, 0.0, halo_ref[j])
    halo_ref[j] = v[tm - SUBLANES:, :]
    vfull = jnp.concatenate([halo, v], axis=0)
    vm1 = pltpu.roll(vfull, 1, 0)[SUBLANES:, :]
    vm2 = pltpu.roll(vfull, 2, 0)[SUBLANES:, :]
    conv = cb_ref[...] + cw_ref[0:1, :] * vm2 + cw_ref[1:2, :] * vm1 + cw_ref[2:3, :] * v
    y = _dot(h, wb_ref[...]) * conv * _silu(_dot(h, wg_ref[...]))
    y_ref[...] = y.astype(BF16)


def _conv_in(x2, mod, norm_g, w_in, conv_w, conv_b, *, seq, tm, cn):
    rows, d = x2.shape
    nb = d // cn
    tiles_per_batch = seq // tm
    kern = functools.partial(_conv_in_kernel, tiles_per_batch=tiles_per_batch)
    w_spec = lambda part: pl.BlockSpec((d, cn), lambda i, j, part=part: (0, part * nb + j))
    return pl.pallas_call(
        kern,
        grid=(rows // tm, nb),
        in_specs=[
            pl.BlockSpec((tm, d), lambda i, j: (i, 0)),
            pl.BlockSpec((1, 3, d), lambda i, j: (i // tiles_per_batch, 0, 0)),
            pl.BlockSpec((1, d), lambda i, j: (0, 0)),
            w_spec(0), w_spec(1), w_spec(2), w_spec(3),
            pl.BlockSpec((CONV_WIDTH, cn), lambda i, j: (0, j)),
            pl.BlockSpec((1, cn), lambda i, j: (0, j)),
        ],
        out_specs=pl.BlockSpec((tm, cn), lambda i, j: (i, j)),
        out_shape=jax.ShapeDtypeStruct((rows, d), BF16),
        scratch_shapes=[
            pltpu.VMEM((tm, d), BF16),
            pltpu.VMEM((nb, SUBLANES, cn), F32),
        ],
        compiler_params=_params("arbitrary", "arbitrary"),
        name="conv_in",
    )(x2, mod, norm_g, w_in, w_in, w_in, w_in, conv_w, conv_b)


def _out_hidden_kernel(y_ref, x_ref, mod_ref, w_ref, nmod_ref, ng_ref, x1_ref, h_ref):
    r = x_ref[...] + mod_ref[0, 2:3, :] * _dot(y_ref[...], w_ref[...])
    x1_ref[...] = r
    h_ref[...] = _norm_modulate(r, ng_ref[...], nmod_ref[0, 0:1, :], nmod_ref[0, 1:2, :]).astype(BF16)


def _out_final_kernel(y_ref, x_ref, mod_ref, w_ref, fg_ref, o_ref):
    r = x_ref[...] + mod_ref[0, 2:3, :] * _dot(y_ref[...], w_ref[...])
    ms = jnp.mean(r * r, axis=-1, keepdims=True)
    o_ref[...] = r * lax.rsqrt(ms + RMS_EPS) * fg_ref[...]


def _out_residual(y2, x2, mod, w, tail, *, seq, tm, name):
    rows, d = x2.shape
    tiles_per_batch = seq // tm
    row_spec = pl.BlockSpec((tm, d), lambda i: (i, 0))
    mod_spec = pl.BlockSpec((1, 3, d), lambda i: (i // tiles_per_batch, 0, 0))
    vec_spec = pl.BlockSpec((1, d), lambda i: (0, 0))
    w_spec = pl.BlockSpec((d, d), lambda i: (0, 0), pipeline_mode=pl.Buffered(1))
    if len(tail) == 2:
        kern, tail_specs = _out_hidden_kernel, [mod_spec, vec_spec]
        out_specs = [row_spec, row_spec]
        out_shape = [jax.ShapeDtypeStruct((rows, d), F32), jax.ShapeDtypeStruct((rows, d), BF16)]
    else:
        kern, tail_specs = _out_final_kernel, [vec_spec]
        out_specs = row_spec
        out_shape = jax.ShapeDtypeStruct((rows, d), F32)
    return pl.pallas_call(
        kern,
        grid=(rows // tm,),
        in_specs=[row_spec, row_spec, mod_spec, w_spec] + tail_specs,
        out_specs=out_specs,
        out_shape=out_shape,
        compiler_params=_params("arbitrary"),
        name=name,
    )(y2, x2, mod, w, *tail)


def _attn_in_kernel(h_ref, w_ref, o_ref, *, q_blocks, q_scale):
    n = pl.program_id(1)
    p = _dot(h_ref[...], w_ref[...])
    p = p * jnp.where(n < q_blocks, q_scale, 1.0)
    dh = o_ref.shape[3]
    for hh in range(o_ref.shape[1]):
        o_ref[0, hh] = p[:, hh * dh:(hh + 1) * dh].astype(BF16)


def _attn_in(h2, w_in, *, bsz, seq, dh, tm, tn):
    rows, d = h2.shape
    n4 = w_in.shape[1]
    tiles_per_batch = seq // tm
    kern = functools.partial(_attn_in_kernel, q_blocks=d // tn, q_scale=LOG2E / math.sqrt(dh))
    return pl.pallas_call(
        kern,
        grid=(rows // tm, n4 // tn),
        in_specs=[
            pl.BlockSpec((tm, d), lambda i, n: (i, 0)),
            pl.BlockSpec((d, tn), lambda i, n: (0, n)),
        ],
        out_specs=pl.BlockSpec(
            (1, tn // dh, tm, dh),
            lambda i, n: (i // tiles_per_batch, n, i % tiles_per_batch, 0)),
        out_shape=jax.ShapeDtypeStruct((bsz, n4 // dh, seq, dh), BF16),
        compiler_params=_params("arbitrary", "arbitrary"),
        name="attn_in",
    )(h2, w_in)


def _stick_attn_kernel(q_ref, k_ref, v_ref, g_ref, u_ref, o_ref, acc_ref, run_ref, *, tk, n_sub):
    qi = pl.program_id(2)
    row = lax.broadcasted_iota(jnp.int32, (tk, tk), 0)
    col = lax.broadcasted_iota(jnp.int32, (tk, tk), 1)
    below_diag = col < row

    def scores(q, kb):
        k0 = pl.multiple_of(kb * tk, tk)
        z2 = lax.dot_general(q, k_ref[0, 0, pl.ds(k0, tk), :], (((1,), (1,)), ((), ())),
                             preferred_element_type=F32)
        nl = jnp.maximum(z2, 0.0) + jnp.log2(1.0 + jnp.exp2(-jnp.abs(z2)))
        return z2, nl

    def suffix_sum(nl):
        hi = nl.astype(BF16)
        lo = (nl - hi.astype(F32)).astype(BF16)
        return _dot(jnp.concatenate([hi, lo], axis=1), u_ref[...])

    def weighted_values(a, kb):
        k0 = pl.multiple_of(kb * tk, tk)
        return _dot(a.astype(BF16), v_ref[0, 0, pl.ds(k0, tk), :])

    run_mins = []
    for s in range(n_sub):
        rows = slice(s * tk, (s + 1) * tk)
        q = q_ref[0, 0, rows, :]
        kb_diag = qi * n_sub + s

        z2, nl = scores(q, kb_diag)
        nl = jnp.where(below_diag, nl, 0.0)
        cum = suffix_sum(nl)
        a = jnp.where(below_diag, jnp.exp2(z2 - cum), 0.0)
        acc = weighted_values(a, kb_diag)
        run = cum[:, 0:1]

        kb_prev = kb_diag - 1
        z2, nl = scores(q, jnp.maximum(kb_prev, 0))
        if s == 0:
            nl = jnp.where(qi > 0, nl, 0.0)
        cum = suffix_sum(nl)
        a = jnp.exp2(z2 - cum - run)
        if s == 0:
            a = jnp.where(qi > 0, a, 0.0)
        acc = acc + weighted_values(a, jnp.maximum(kb_prev, 0))
        run = run + cum[:, 0:1]
        acc_ref[s] = acc
        run_ref[s] = run
        run_mins.append(jnp.min(run))
        o_ref[0, rows, :] = (acc * _silu(g_ref[0, 0, rows, :].astype(F32))).astype(BF16)

    def more(carry):
        kb, run_min = carry
        return jnp.logical_and(kb >= 0, run_min < EXP2_ZERO_ABOVE)

    for s in range(n_sub):
        start = (qi * n_sub + s - 2, run_mins[s])

        @pl.when(more(start))
        def _(s=s, start=start):
            rows = slice(s * tk, (s + 1) * tk)
            q = q_ref[0, 0, rows, :]

            def visit(carry):
                kb, _ = carry
                z2, nl = scores(q, kb)
                cum = suffix_sum(nl)
                run = run_ref[s]
                acc_ref[s] += weighted_values(jnp.exp2(z2 - cum - run), kb)
                run = run + cum[:, 0:1]
                run_ref[s] = run
                return kb - 1, jnp.min(run)

            lax.while_loop(more, visit, start)
            o_ref[0, rows, :] = (acc_ref[s] * _silu(g_ref[0, 0, rows, :].astype(F32))).astype(BF16)


def _stick_attn(qkvg, *, n_heads, tk, n_sub):
    bsz, _, seq, dh = qkvg.shape
    tq = tk * n_sub
    tri = (lax.broadcasted_iota(jnp.int32, (tk, tk), 0)
           >= lax.broadcasted_iota(jnp.int32, (tk, tk), 1)).astype(BF16)
    u2 = jnp.concatenate([tri, tri], axis=0)
    kern = functools.partial(_stick_attn_kernel, tk=tk, n_sub=n_sub)
    return pl.pallas_call(
        kern,
        grid=(bsz, n_heads, seq // tq),
        in_specs=[
            pl.BlockSpec((1, 1, tq, dh), lambda b, h, qi: (b, h, qi, 0)),
            pl.BlockSpec((1, 1, seq, dh), lambda b, h, qi: (b, n_heads + h, 0, 0)),
            pl.BlockSpec((1, 1, seq, dh), lambda b, h, qi: (b, 2 * n_heads + h, 0, 0)),
            pl.BlockSpec((1, 1, tq, dh), lambda b, h, qi: (b, 3 * n_heads + h, qi, 0)),
            pl.BlockSpec((2 * tk, tk), lambda b, h, qi: (0, 0), pipeline_mode=pl.Buffered(1)),
        ],
        out_specs=pl.BlockSpec((1, tq, dh), lambda b, h, qi: (b, qi, h)),
        out_shape=jax.ShapeDtypeStruct((bsz, seq, n_heads * dh), BF16),
        scratch_shapes=[
            pltpu.VMEM((n_sub, tk, dh), F32),
            pltpu.VMEM((n_sub, tk, 1), F32),
        ],
        compiler_params=_params("arbitrary", "arbitrary", "arbitrary"),
        name="stick_attn",
    )(qkvg, qkvg, qkvg, qkvg, u2)


def _tiles(seq):
    return dict(
        tm_in=min(1024, seq),
        tm_out=min(512, seq),
        n_sub=min(4, seq // MXU_DIM),
    )


def kernel(x, c, norm_g, ada_w, ada_b, conv_in_w, conv_w, conv_b, conv_out_w, attn_in_w, attn_out_w, final_g):
    bsz, seq, d = x.shape
    dh = d // N_HEADS
    t = _tiles(seq)
    x2 = x.reshape(bsz * seq, d)
    mod = _adaln_mod(c, ada_w, ada_b)
    mod0 = mod[0].reshape(bsz, 3, d)
    mod1 = mod[1].reshape(bsz, 3, d)

    y0 = _conv_in(x2, mod0, norm_g[0:1], conv_in_w[0].astype(BF16), conv_w[0], conv_b[0:1],
                  seq=seq, tm=t["tm_in"], cn=MXU_DIM)
    x1, h1 = _out_residual(y0, x2, mod0, conv_out_w[0].astype(BF16), (mod1, norm_g[1:2]),
                           seq=seq, tm=t["tm_out"], name="conv_out")
    qkvg = _attn_in(h1, attn_in_w[0].astype(BF16), bsz=bsz, seq=seq, dh=dh, tm=t["tm_in"], tn=1024)
    o = _stick_attn(qkvg, n_heads=N_HEADS, tk=MXU_DIM, n_sub=t["n_sub"])
    out = _out_residual(o.reshape(bsz * seq, d), x1, mod1, attn_out_w[0].astype(BF16),
                        (final_g.reshape(1, d),), seq=seq, tm=t["tm_out"], name="attn_out")
    return out.reshape(bsz, seq, d)
```

```python
import functools
import math

import jax
import jax.numpy as jnp
from jax import lax
from jax.experimental import pallas as pl
from jax.experimental.pallas import tpu as pltpu

N_HEADS = 8
CONV_WIDTH = 3
RMS_EPS = 1e-6

SUBLANES = 8
MXU_DIM = 256
MIB = 1024 * 1024
VMEM_LIMIT = 56 * MIB

EXP2_ZERO_ABOVE = 151.0
LOG2E = 1.4426950408889634

F32 = jnp.float32
BF16 = jnp.bfloat16


def _silu(x):
    return x / (1.0 + jnp.exp(-x))


def _dot(a, b):
    return jnp.dot(a, b, preferred_element_type=F32)


def _norm_modulate(x, norm_g, shift, scale):
    ms = jnp.mean(x * x, axis=-1, keepdims=True)
    return (x * lax.rsqrt(ms + RMS_EPS) * norm_g) * (1.0 + scale) + shift


def _params(*semantics):
    return pltpu.CompilerParams(dimension_semantics=semantics, vmem_limit_bytes=VMEM_LIMIT)


def _adaln_kernel(c_ref, w_ref, b_ref, o_ref):
    c_act = _silu(c_ref[...])
    o_ref[0] = jnp.dot(c_act, w_ref[0], preferred_element_type=F32,
                       precision=lax.Precision.HIGHEST) + b_ref[0]


def _adaln_mod(c, ada_w, ada_b):
    depth, d, n3 = ada_w.shape
    bsz = c.shape[0]
    tn = 768
    return pl.pallas_call(
        _adaln_kernel,
        grid=(depth, n3 // tn),
        in_specs=[
            pl.BlockSpec((bsz, d), lambda l, n: (0, 0)),
            pl.BlockSpec((1, d, tn), lambda l, n: (l, 0, n)),
            pl.BlockSpec((1, 1, tn), lambda l, n: (l, 0, n)),
        ],
        out_specs=pl.BlockSpec((1, bsz, tn), lambda l, n: (l, 0, n)),
        out_shape=jax.ShapeDtypeStruct((depth, bsz, n3), F32),
        compiler_params=_params("arbitrary", "arbitrary"),
        name="adaln_mod",
    )(c, ada_w, ada_b.reshape(depth, 1, n3))


def _norm_rows_kernel(x_ref, mod_ref, ng_ref, o_ref):
    o_ref[...] = _norm_modulate(x_ref[...], ng_ref[...], mod_ref[0, 0:1, :], mod_ref[0, 1:2, :]).astype(BF16)


def _norm_first_tile(x2, mod, norm_g, *, tm):
    d = x2.shape[1]
    return pl.pallas_call(
        _norm_rows_kernel,
        grid=(1,),
        in_specs=[
            pl.BlockSpec((tm, d), lambda i: (0, 0)),
            pl.BlockSpec((1, 3, d), lambda i: (0, 0, 0)),
            pl.BlockSpec((1, d), lambda i: (0, 0)),
        ],
        out_specs=pl.BlockSpec((tm, d), lambda i: (0, 0)),
        out_shape=jax.ShapeDtypeStruct((tm, d), BF16),
        compiler_params=_params("arbitrary"),
        name="norm_first",
    )(x2, mod, norm_g)


def _conv_in_kernel(xn_ref, modn_ref, ng_ref, hfirst_ref, wb_ref, wc_ref, wu_ref, wg_ref, cw_ref, cb_ref,
                    y_ref, ha_ref, hb_ref, halo_ref, *, tiles_per_batch, chunk):
    i = pl.program_id(0)
    j = pl.program_id(1)
    tm = y_ref.shape[0]

    @pl.when(jnp.logical_and(i == 0, j == 0))
    def _():
        ha_ref[...] = hfirst_ref[...]

    def step(h_ref, hnext_ref):
        r0 = pl.multiple_of(j * chunk, chunk)
        hn = _norm_modulate(xn_ref[pl.ds(r0, chunk), :], ng_ref[...], modn_ref[0, 0:1, :], modn_ref[0, 1:2, :])
        hnext_ref[pl.ds(r0, chunk), :] = hn.astype(BF16)

        h = h_ref[...]
        v = _dot(h, wc_ref[...]) * _dot(h, wu_ref[...])
        first = (i % tiles_per_batch) == 0
        halo = jnp.where(first, 0.0, halo_ref[j])
        halo_ref[j] = v[tm - SUBLANES:, :]
        vfull = jnp.concatenate([halo, v], axis=0)
        vm1 = pltpu.roll(vfull, 1, 0)[SUBLANES:, :]
        vm2 = pltpu.roll(vfull, 2, 0)[SUBLANES:, :]
        conv = cb_ref[...] + cw_ref[0:1, :] * vm2 + cw_ref[1:2, :] * vm1 + cw_ref[2:3, :] * v
        y = _dot(h, wb_ref[...]) * conv * _silu(_dot(h, wg_ref[...]))
        y_ref[...] = y.astype(BF16)

    pl.when(i % 2 == 0)(functools.partial(step, ha_ref, hb_ref))
    pl.when(i % 2 == 1)(functools.partial(step, hb_ref, ha_ref))


def _conv_in(x2, mod, norm_g, w_in, conv_w, conv_b, *, seq, tm, cn):
    rows, d = x2.shape
    nb = d // cn
    n_tiles = rows // tm
    tiles_per_batch = seq // tm
    h_first = _norm_first_tile(x2, mod, norm_g, tm=tm)
    kern = functools.partial(_conv_in_kernel, tiles_per_batch=tiles_per_batch, chunk=tm // nb)
    w_spec = lambda part: pl.BlockSpec((d, cn), lambda i, j, part=part: (0, part * nb + j))
    next_tile = lambda i: jnp.minimum(i + 1, n_tiles - 1)
    return pl.pallas_call(
        kern,
        grid=(n_tiles, nb),
        in_specs=[
            pl.BlockSpec((tm, d), lambda i, j: (next_tile(i), 0)),
            pl.BlockSpec((1, 3, d), lambda i, j: (next_tile(i) // tiles_per_batch, 0, 0)),
            pl.BlockSpec((1, d), lambda i, j: (0, 0)),
            pl.BlockSpec((tm, d), lambda i, j: (0, 0), pipeline_mode=pl.Buffered(1)),
            w_spec(0), w_spec(1), w_spec(2), w_spec(3),
            pl.BlockSpec((CONV_WIDTH, cn), lambda i, j: (0, j)),
            pl.BlockSpec((1, cn), lambda i, j: (0, j)),
        ],
        out_specs=pl.BlockSpec((tm, cn), lambda i, j: (i, j)),
        out_shape=jax.ShapeDtypeStruct((rows, d), BF16),
        scratch_shapes=[
            pltpu.VMEM((tm, d), BF16),
            pltpu.VMEM((tm, d), BF16),
            pltpu.VMEM((nb, SUBLANES, cn), F32),
        ],
        compiler_params=_params("arbitrary", "arbitrary"),
        name="conv_in",
    )(x2, mod, norm_g, h_first, w_in, w_in, w_in, w_in, conv_w, conv_b)


def _out_hidden_kernel(y_ref, x_ref, mod_ref, w_ref, nmod_ref, ng_ref, x1_ref, h_ref):
    r = x_ref[...] + mod_ref[0, 2:3, :] * _dot(y_ref[...], w_ref[...])
    x1_ref[...] = r
    h_ref[...] = _norm_modulate(r, ng_ref[...], nmod_ref[0, 0:1, :], nmod_ref[0, 1:2, :]).astype(BF16)


def _out_final_kernel(y_ref, x_ref, mod_ref, w_ref, fg_ref, o_ref):
    r = x_ref[...] + mod_ref[0, 2:3, :] * _dot(y_ref[...], w_ref[...])
    ms = jnp.mean(r * r, axis=-1, keepdims=True)
    o_ref[...] = r * lax.rsqrt(ms + RMS_EPS) * fg_ref[...]


def _out_residual(y2, x2, mod, w, tail, *, seq, tm, name):
    rows, d = x2.shape
    tiles_per_batch = seq // tm
    row_spec = pl.BlockSpec((tm, d), lambda i: (i, 0))
    mod_spec = pl.BlockSpec((1, 3, d), lambda i: (i // tiles_per_batch, 0, 0))
    vec_spec = pl.BlockSpec((1, d), lambda i: (0, 0))
    w_spec = pl.BlockSpec((d, d), lambda i: (0, 0), pipeline_mode=pl.Buffered(1))
    if len(tail) == 2:
        kern, tail_specs = _out_hidden_kernel, [mod_spec, vec_spec]
        out_specs = [row_spec, row_spec]
        out_shape = [jax.ShapeDtypeStruct((rows, d), F32), jax.ShapeDtypeStruct((rows, d), BF16)]
    else:
        kern, tail_specs = _out_final_kernel, [vec_spec]
        out_specs = row_spec
        out_shape = jax.ShapeDtypeStruct((rows, d), F32)
    return pl.pallas_call(
        kern,
        grid=(rows // tm,),
        in_specs=[row_spec, row_spec, mod_spec, w_spec] + tail_specs,
        out_specs=out_specs,
        out_shape=out_shape,
        compiler_params=_params("arbitrary"),
        name=name,
    )(y2, x2, mod, w, *tail)


def _attn_in_kernel(h_ref, w_ref, o_ref, *, q_blocks, q_scale):
    n = pl.program_id(1)
    p = _dot(h_ref[...], w_ref[...])
    p = p * jnp.where(n < q_blocks, q_scale, 1.0)
    dh = o_ref.shape[3]
    for hh in range(o_ref.shape[1]):
        o_ref[0, hh] = p[:, hh * dh:(hh + 1) * dh].astype(BF16)


def _attn_in(h2, w_in, *, bsz, seq, dh, tm, tn):
    rows, d = h2.shape
    n4 = w_in.shape[1]
    tiles_per_batch = seq // tm
    kern = functools.partial(_attn_in_kernel, q_blocks=d // tn, q_scale=LOG2E / math.sqrt(dh))
    return pl.pallas_call(
        kern,
        grid=(rows // tm, n4 // tn),
        in_specs=[
            pl.BlockSpec((tm, d), lambda i, n: (i, 0)),
            pl.BlockSpec((d, tn), lambda i, n: (0, n)),
        ],
        out_specs=pl.BlockSpec(
            (1, tn // dh, tm, dh),
            lambda i, n: (i // tiles_per_batch, n, i % tiles_per_batch, 0)),
        out_shape=jax.ShapeDtypeStruct((bsz, n4 // dh, seq, dh), BF16),
        compiler_params=_params("arbitrary", "arbitrary"),
        name="attn_in",
    )(h2, w_in)


def _stick_attn_kernel(q_ref, k_ref, v_ref, g_ref, u_ref, o_ref, acc_ref, run_ref, *, tk, n_sub):
    qi = pl.program_id(2)
    row = lax.broadcasted_iota(jnp.int32, (tk, tk), 0)
    col = lax.broadcasted_iota(jnp.int32, (tk, tk), 1)
    below_diag = col < row

    def scores(q, kb):
        k0 = pl.multiple_of(kb * tk, tk)
        z2 = lax.dot_general(q, k_ref[0, 0, pl.ds(k0, tk), :], (((1,), (1,)), ((), ())),
                             preferred_element_type=F32)
        nl = jnp.maximum(z2, 0.0) + jnp.log2(1.0 + jnp.exp2(-jnp.abs(z2)))
        return z2, nl

    def suffix_sum(nl):
        return _dot(nl.astype(BF16), u_ref[...])

    def weighted_values(a, kb):
        k0 = pl.multiple_of(kb * tk, tk)
        return _dot(a.astype(BF16), v_ref[0, 0, pl.ds(k0, tk), :])

    run_mins = []
    for s in range(n_sub):
        rows = slice(s * tk, (s + 1) * tk)
        q = q_ref[0, 0, rows, :]
        kb_diag = qi * n_sub + s

        z2, nl = scores(q, kb_diag)
        nl = jnp.where(below_diag, nl, 0.0)
        cum = suffix_sum(nl)
        a = jnp.where(below_diag, jnp.exp2(z2 - cum), 0.0)
        acc = weighted_values(a, kb_diag)
        run = cum[:, 0:1]

        kb_prev = kb_diag - 1
        z2, nl = scores(q, jnp.maximum(kb_prev, 0))
        if s == 0:
            nl = jnp.where(qi > 0, nl, 0.0)
        cum = suffix_sum(nl)
        a = jnp.exp2(z2 - cum - run)
        if s == 0:
            a = jnp.where(qi > 0, a, 0.0)
        acc = acc + weighted_values(a, jnp.maximum(kb_prev, 0))
        run = run + cum[:, 0:1]
        acc_ref[s] = acc
        run_ref[s] = run
        run_mins.append(jnp.min(run))
        o_ref[0, rows, :] = (acc * _silu(g_ref[0, 0, rows, :].astype(F32))).astype(BF16)

    def more(carry):
        kb, run_min = carry
        return jnp.logical_and(kb >= 0, run_min < EXP2_ZERO_ABOVE)

    for s in range(n_sub):
        start = (qi * n_sub + s - 2, run_mins[s])

        @pl.when(more(start))
        def _(s=s, start=start):
            rows = slice(s * tk, (s + 1) * tk)
            q = q_ref[0, 0, rows, :]

            def visit(carry):
                kb, _ = carry
                z2, nl = scores(q, kb)
                cum = suffix_sum(nl)
                run = run_ref[s]
                acc_ref[s] += weighted_values(jnp.exp2(z2 - cum - run), kb)
                run = run + cum[:, 0:1]
                run_ref[s] = run
                return kb - 1, jnp.min(run)

            lax.while_loop(more, visit, start)
            o_ref[0, rows, :] = (acc_ref[s] * _silu(g_ref[0, 0, rows, :].astype(F32))).astype(BF16)


def _stick_attn(qkvg, *, n_heads, tk, n_sub):
    bsz, _, seq, dh = qkvg.shape
    tq = tk * n_sub
    tri = (lax.broadcasted_iota(jnp.int32, (tk, tk), 0)
           >= lax.broadcasted_iota(jnp.int32, (tk, tk), 1)).astype(BF16)
    kern = functools.partial(_stick_attn_kernel, tk=tk, n_sub=n_sub)
    return pl.pallas_call(
        kern,
        grid=(bsz, n_heads, seq // tq),
        in_specs=[
            pl.BlockSpec((1, 1, tq, dh), lambda b, h, qi: (b, h, qi, 0)),
            pl.BlockSpec((1, 1, seq, dh), lambda b, h, qi: (b, n_heads + h, 0, 0)),
            pl.BlockSpec((1, 1, seq, dh), lambda b, h, qi: (b, 2 * n_heads + h, 0, 0)),
            pl.BlockSpec((1, 1, tq, dh), lambda b, h, qi: (b, 3 * n_heads + h, qi, 0)),
            pl.BlockSpec((tk, tk), lambda b, h, qi: (0, 0), pipeline_mode=pl.Buffered(1)),
        ],
        out_specs=pl.BlockSpec((1, tq, dh), lambda b, h, qi: (b, qi, h)),
        out_shape=jax.ShapeDtypeStruct((bsz, seq, n_heads * dh), BF16),
        scratch_shapes=[
            pltpu.VMEM((n_sub, tk, dh), F32),
            pltpu.VMEM((n_sub, tk, 1), F32),
        ],
        compiler_params=_params("arbitrary", "arbitrary", "arbitrary"),
        name="stick_attn",
    )(qkvg, qkvg, qkvg, qkvg, tri)


def _tiles(seq):
    return dict(
        tm_in=min(1024, seq),
        tm_out=min(512, seq),
        cn=MXU_DIM,
        tn=2048,
        n_sub=min(8, seq // MXU_DIM),
    )


def kernel(x, c, norm_g, ada_w, ada_b, conv_in_w, conv_w, conv_b, conv_out_w, attn_in_w, attn_out_w, final_g):
    bsz, seq, d = x.shape
    dh = d // N_HEADS
    t = _tiles(seq)
    x2 = x.reshape(bsz * seq, d)
    mod = _adaln_mod(c, ada_w, ada_b)
    mod0 = mod[0].reshape(bsz, 3, d)
    mod1 = mod[1].reshape(bsz, 3, d)

    y0 = _conv_in(x2, mod0, norm_g[0:1], conv_in_w[0].astype(BF16), conv_w[0], conv_b[0:1],
                  seq=seq, tm=t["tm_in"], cn=t["cn"])
    x1, h1 = _out_residual(y0, x2, mod0, conv_out_w[0].astype(BF16), (mod1, norm_g[1:2]),
                           seq=seq, tm=t["tm_out"], name="conv_out")
    qkvg = _attn_in(h1, attn_in_w[0].astype(BF16), bsz=bsz, seq=seq, dh=dh, tm=t["tm_in"], tn=t["tn"])
    o = _stick_attn(qkvg, n_heads=N_HEADS, tk=MXU_DIM, n_sub=t["n_sub"])
    out = _out_residual(o.reshape(bsz * seq, d), x1, mod1, attn_out_w[0].astype(BF16),
                        (final_g.reshape(1, d),), seq=seq, tm=t["tm_out"], name="attn_out")
    return out.reshape(bsz, seq, d)
```

```python
import functools
import math

import jax
import jax.numpy as jnp
from jax import lax
from jax.experimental import pallas as pl
from jax.experimental.pallas import tpu as pltpu

N_HEADS = 8
CONV_WIDTH = 3
RMS_EPS = 1e-6

SUBLANES = 8
MXU_DIM = 256
MIB = 1024 * 1024
VMEM_LIMIT = 56 * MIB

EXP2_ZERO_ABOVE = 151.0
LOG2E = 1.4426950408889634

F32 = jnp.float32
BF16 = jnp.bfloat16


def _silu(x):
    return x / (1.0 + jnp.exp(-x))


def _dot(a, b):
    return jnp.dot(a, b, preferred_element_type=F32)


def _norm_modulate(x, norm_g, shift, scale):
    ms = jnp.mean(x * x, axis=-1, keepdims=True)
    return (x * lax.rsqrt(ms + RMS_EPS) * norm_g) * (1.0 + scale) + shift


def _params(*semantics):
    return pltpu.CompilerParams(dimension_semantics=semantics, vmem_limit_bytes=VMEM_LIMIT)


def _adaln_kernel(c_ref, w_ref, b_ref, o_ref):
    c_act = _silu(c_ref[...])
    o_ref[0] = jnp.dot(c_act, w_ref[0], preferred_element_type=F32,
                       precision=lax.Precision.HIGHEST) + b_ref[0]


def _adaln_mod(c, ada_w, ada_b):
    depth, d, n3 = ada_w.shape
    bsz = c.shape[0]
    tn = 768
    return pl.pallas_call(
        _adaln_kernel,
        grid=(depth, n3 // tn),
        in_specs=[
            pl.BlockSpec((bsz, d), lambda l, n: (0, 0)),
            pl.BlockSpec((1, d, tn), lambda l, n: (l, 0, n)),
            pl.BlockSpec((1, 1, tn), lambda l, n: (l, 0, n)),
        ],
        out_specs=pl.BlockSpec((1, bsz, tn), lambda l, n: (l, 0, n)),
        out_shape=jax.ShapeDtypeStruct((depth, bsz, n3), F32),
        compiler_params=_params("arbitrary", "arbitrary"),
        name="adaln_mod",
    )(c, ada_w, ada_b.reshape(depth, 1, n3))


def _norm_rows_kernel(x_ref, mod_ref, ng_ref, o_ref):
    o_ref[...] = _norm_modulate(x_ref[...], ng_ref[...], mod_ref[0, 0:1, :], mod_ref[0, 1:2, :]).astype(BF16)


def _norm_first_tile(x2, mod, norm_g, *, tm):
    d = x2.shape[1]
    return pl.pallas_call(
        _norm_rows_kernel,
        grid=(1,),
        in_specs=[
            pl.BlockSpec((tm, d), lambda i: (0, 0)),
            pl.BlockSpec((1, 3, d), lambda i: (0, 0, 0)),
            pl.BlockSpec((1, d), lambda i: (0, 0)),
        ],
        out_specs=pl.BlockSpec((tm, d), lambda i: (0, 0)),
        out_shape=jax.ShapeDtypeStruct((tm, d), BF16),
        compiler_params=_params("arbitrary"),
        name="norm_first",
    )(x2, mod, norm_g)


def _conv_in_kernel(xn_ref, modn_ref, ng_ref, hfirst_ref, wb_ref, wc_ref, wu_ref, wg_ref, cw_ref, cb_ref,
                    y_ref, ha_ref, hb_ref, halo_ref, *, tiles_per_batch, chunk):
    i = pl.program_id(0)
    j = pl.program_id(1)
    tm = y_ref.shape[0]

    @pl.when(jnp.logical_and(i == 0, j == 0))
    def _():
        ha_ref[...] = hfirst_ref[...]

    def step(h_ref, hnext_ref):
        r0 = pl.multiple_of(j * chunk, chunk)
        hn = _norm_modulate(xn_ref[pl.ds(r0, chunk), :], ng_ref[...], modn_ref[0, 0:1, :], modn_ref[0, 1:2, :])
        hnext_ref[pl.ds(r0, chunk), :] = hn.astype(BF16)

        h = h_ref[...]
        v = _dot(h, wc_ref[...]) * _dot(h, wu_ref[...])
        first = (i % tiles_per_batch) == 0
        halo = jnp.where(first, 0.0, halo_ref[j])
        halo_ref[j] = v[tm - SUBLANES:, :]
        vfull = jnp.concatenate([halo, v], axis=0)
        vm1 = pltpu.roll(vfull, 1, 0)[SUBLANES:, :]
        vm2 = pltpu.roll(vfull, 2, 0)[SUBLANES:, :]
        conv = cb_ref[...] + cw_ref[0:1, :] * vm2 + cw_ref[1:2, :] * vm1 + cw_ref[2:3, :] * v
        y = _dot(h, wb_ref[...]) * conv * _silu(_dot(h, wg_ref[...]))
        y_ref[...] = y.astype(BF16)

    pl.when(i % 2 == 0)(functools.partial(step, ha_ref, hb_ref))
    pl.when(i % 2 == 1)(functools.partial(step, hb_ref, ha_ref))


def _conv_in(x2, mod, norm_g, w_in, conv_w, conv_b, *, seq, tm, cn):
    rows, d = x2.shape
    nb = d // cn
    n_tiles = rows // tm
    tiles_per_batch = seq // tm
    h_first = _norm_first_tile(x2, mod, norm_g, tm=tm)
    kern = functools.partial(_conv_in_kernel, tiles_per_batch=tiles_per_batch, chunk=tm // nb)
    w_spec = lambda part: pl.BlockSpec((d, cn), lambda i, j, part=part: (0, part * nb + j))
    next_tile = lambda i: jnp.minimum(i + 1, n_tiles - 1)
    return pl.pallas_call(
        kern,
        grid=(n_tiles, nb),
        in_specs=[
            pl.BlockSpec((tm, d), lambda i, j: (next_tile(i), 0)),
            pl.BlockSpec((1, 3, d), lambda i, j: (next_tile(i) // tiles_per_batch, 0, 0)),
            pl.BlockSpec((1, d), lambda i, j: (0, 0)),
            pl.BlockSpec((tm, d), lambda i, j: (0, 0), pipeline_mode=pl.Buffered(1)),
            w_spec(0), w_spec(1), w_spec(2), w_spec(3),
            pl.BlockSpec((CONV_WIDTH, cn), lambda i, j: (0, j)),
            pl.BlockSpec((1, cn), lambda i, j: (0, j)),
        ],
        out_specs=pl.BlockSpec((tm, cn), lambda i, j: (i, j)),
        out_shape=jax.ShapeDtypeStruct((rows, d), BF16),
        scratch_shapes=[
            pltpu.VMEM((tm, d), BF16),
            pltpu.VMEM((tm, d), BF16),
            pltpu.VMEM((nb, SUBLANES, cn), F32),
        ],
        compiler_params=_params("arbitrary", "arbitrary"),
        name="conv_in",
    )(x2, mod, norm_g, h_first, w_in, w_in, w_in, w_in, conv_w, conv_b)


def _out_hidden_kernel(y_ref, x_ref, mod_ref, w_ref, nmod_ref, ng_ref, x1_ref, h_ref):
    r = x_ref[...] + mod_ref[0, 2:3, :] * _dot(y_ref[...], w_ref[...])
    x1_ref[...] = r
    h_ref[...] = _norm_modulate(r, ng_ref[...], nmod_ref[0, 0:1, :], nmod_ref[0, 1:2, :]).astype(BF16)


def _out_final_kernel(y_ref, x_ref, mod_ref, w_ref, fg_ref, o_ref):
    r = x_ref[...] + mod_ref[0, 2:3, :] * _dot(y_ref[...], w_ref[...])
    ms = jnp.mean(r * r, axis=-1, keepdims=True)
    o_ref[...] = r * lax.rsqrt(ms + RMS_EPS) * fg_ref[...]


def _out_residual(y2, x2, mod, w, tail, *, seq, tm, name):
    rows, d = x2.shape
    tiles_per_batch = seq // tm
    row_spec = pl.BlockSpec((tm, d), lambda i: (i, 0))
    mod_spec = pl.BlockSpec((1, 3, d), lambda i: (i // tiles_per_batch, 0, 0))
    vec_spec = pl.BlockSpec((1, d), lambda i: (0, 0))
    w_spec = pl.BlockSpec((d, d), lambda i: (0, 0), pipeline_mode=pl.Buffered(1))
    if len(tail) == 2:
        kern, tail_specs = _out_hidden_kernel, [mod_spec, vec_spec]
        out_specs = [row_spec, row_spec]
        out_shape = [jax.ShapeDtypeStruct((rows, d), F32), jax.ShapeDtypeStruct((rows, d), BF16)]
    else:
        kern, tail_specs = _out_final_kernel, [vec_spec]
        out_specs = row_spec
        out_shape = jax.ShapeDtypeStruct((rows, d), F32)
    return pl.pallas_call(
        kern,
        grid=(rows // tm,),
        in_specs=[row_spec, row_spec, mod_spec, w_spec] + tail_specs,
        out_specs=out_specs,
        out_shape=out_shape,
        compiler_params=_params("arbitrary"),
        name=name,
    )(y2, x2, mod, w, *tail)


def _attn_in_kernel(h_ref, w_ref, o_ref, *, q_blocks, q_scale):
    n = pl.program_id(1)
    p = _dot(h_ref[...], w_ref[...])
    p = p * jnp.where(n < q_blocks, q_scale, 1.0)
    dh = o_ref.shape[3]
    for hh in range(o_ref.shape[1]):
        o_ref[0, hh] = p[:, hh * dh:(hh + 1) * dh].astype(BF16)


def _attn_in(h2, w_in, *, bsz, seq, dh, tm, tn):
    rows, d = h2.shape
    n4 = w_in.shape[1]
    tiles_per_batch = seq // tm
    kern = functools.partial(_attn_in_kernel, q_blocks=d // tn, q_scale=LOG2E / math.sqrt(dh))
    return pl.pallas_call(
        kern,
        grid=(rows // tm, n4 // tn),
        in_specs=[
            pl.BlockSpec((tm, d), lambda i, n: (i, 0)),
            pl.BlockSpec((d, tn), lambda i, n: (0, n)),
        ],
        out_specs=pl.BlockSpec(
            (1, tn // dh, tm, dh),
            lambda i, n: (i // tiles_per_batch, n, i % tiles_per_batch, 0)),
        out_shape=jax.ShapeDtypeStruct((bsz, n4 // dh, seq, dh), BF16),
        compiler_params=_params("arbitrary", "arbitrary"),
        name="attn_in",
    )(h2, w_in)


def _stick_attn_kernel(q_ref, k_ref, v_ref, g_ref, u_ref, o_ref, acc_ref, run_ref, *, tk, n_sub):
    qi = pl.program_id(2)
    row = lax.broadcasted_iota(jnp.int32, (tk, tk), 0)
    col = lax.broadcasted_iota(jnp.int32, (tk, tk), 1)
    below_diag = col < row

    def scores(q, kb):
        k0 = pl.multiple_of(kb * tk, tk)
        z2 = lax.dot_general(q, k_ref[0, 0, pl.ds(k0, tk), :], (((1,), (1,)), ((), ())),
                             preferred_element_type=F32)
        nl = jnp.maximum(z2, 0.0) + jnp.log2(1.0 + jnp.exp2(-jnp.abs(z2)))
        return z2, nl

    def suffix_sum(nl):
        hi = nl.astype(BF16)
        lo = (nl - hi.astype(F32)).astype(BF16)
        return _dot(jnp.concatenate([hi, lo], axis=1), u_ref[...])

    def weighted_values(a, kb):
        k0 = pl.multiple_of(kb * tk, tk)
        return _dot(a.astype(BF16), v_ref[0, 0, pl.ds(k0, tk), :])

    run_mins = []
    for s in range(n_sub):
        rows = slice(s * tk, (s + 1) * tk)
        q = q_ref[0, 0, rows, :]
        kb_diag = qi * n_sub + s

        z2, nl = scores(q, kb_diag)
        nl = jnp.where(below_diag, nl, 0.0)
        cum = suffix_sum(nl)
        a = jnp.where(below_diag, jnp.exp2(z2 - cum), 0.0)
        acc = weighted_values(a, kb_diag)
        run = cum[:, 0:1]

        kb_prev = kb_diag - 1
        z2, nl = scores(q, jnp.maximum(kb_prev, 0))
        if s == 0:
            nl = jnp.where(qi > 0, nl, 0.0)
        cum = suffix_sum(nl)
        a = jnp.exp2(z2 - cum - run)
        if s == 0:
            a = jnp.where(qi > 0, a, 0.0)
        acc = acc + weighted_values(a, jnp.maximum(kb_prev, 0))
        run = run + cum[:, 0:1]
        acc_ref[s] = acc
        run_ref[s] = run
        run_mins.append(jnp.min(run))
        o_ref[0, rows, :] = (acc * _silu(g_ref[0, 0, rows, :].astype(F32))).astype(BF16)

    def more(carry):
        kb, run_min = carry
        return jnp.logical_and(kb >= 0, run_min < EXP2_ZERO_ABOVE)

    for s in range(n_sub):
        start = (qi * n_sub + s - 2, run_mins[s])

        @pl.when(more(start))
        def _(s=s, start=start):
            rows = slice(s * tk, (s + 1) * tk)
            q = q_ref[0, 0, rows, :]

            def visit(carry):
                kb, _ = carry
                z2, nl = scores(q, kb)
                cum = suffix_sum(nl)
                run = run_ref[s]
                acc_ref[s] += weighted_values(jnp.exp2(z2 - cum - run), kb)
                run = run + cum[:, 0:1]
                run_ref[s] = run
                return kb - 1, jnp.min(run)

            lax.while_loop(more, visit, start)
            o_ref[0, rows, :] = (acc_ref[s] * _silu(g_ref[0, 0, rows, :].astype(F32))).astype(BF16)


def _stick_attn(qkvg, *, n_heads, tk, n_sub):
    bsz, _, seq, dh = qkvg.shape
    tq = tk * n_sub
    tri = (lax.broadcasted_iota(jnp.int32, (tk, tk), 0)
           >= lax.broadcasted_iota(jnp.int32, (tk, tk), 1)).astype(BF16)
    tri = jnp.concatenate([tri, tri], axis=0)
    kern = functools.partial(_stick_attn_kernel, tk=tk, n_sub=n_sub)
    return pl.pallas_call(
        kern,
        grid=(bsz, n_heads, seq // tq),
        in_specs=[
            pl.BlockSpec((1, 1, tq, dh), lambda b, h, qi: (b, h, qi, 0)),
            pl.BlockSpec((1, 1, seq, dh), lambda b, h, qi: (b, n_heads + h, 0, 0)),
            pl.BlockSpec((1, 1, seq, dh), lambda b, h, qi: (b, 2 * n_heads + h, 0, 0)),
            pl.BlockSpec((1, 1, tq, dh), lambda b, h, qi: (b, 3 * n_heads + h, qi, 0)),
            pl.BlockSpec((2 * tk, tk), lambda b, h, qi: (0, 0), pipeline_mode=pl.Buffered(1)),
        ],
        out_specs=pl.BlockSpec((1, tq, dh), lambda b, h, qi: (b, qi, h)),
        out_shape=jax.ShapeDtypeStruct((bsz, seq, n_heads * dh), BF16),
        scratch_shapes=[
            pltpu.VMEM((n_sub, tk, dh), F32),
            pltpu.VMEM((n_sub, tk, 1), F32),
        ],
        compiler_params=_params("arbitrary", "arbitrary", "arbitrary"),
        name="stick_attn",
    )(qkvg, qkvg, qkvg, qkvg, tri)


def _tiles(seq):
    return dict(
        tm_conv_in=min(1024, seq),
        cn=2 * MXU_DIM,
        tm_conv_out=min(512, seq),
        tm_attn_in=min(1024, seq),
        tn=2048,
        tm_attn_out=min(512, seq),
        n_sub=min(8, seq // MXU_DIM),
    )


def kernel(x, c, norm_g, ada_w, ada_b, conv_in_w, conv_w, conv_b, conv_out_w, attn_in_w, attn_out_w, final_g):
    bsz, seq, d = x.shape
    dh = d // N_HEADS
    t = _tiles(seq)
    x2 = x.reshape(bsz * seq, d)
    mod = _adaln_mod(c, ada_w, ada_b)
    mod0 = mod[0].reshape(bsz, 3, d)
    mod1 = mod[1].reshape(bsz, 3, d)

    y0 = _conv_in(x2, mod0, norm_g[0:1], conv_in_w[0].astype(BF16), conv_w[0], conv_b[0:1],
                  seq=seq, tm=t["tm_conv_in"], cn=t["cn"])
    x1, h1 = _out_residual(y0, x2, mod0, conv_out_w[0].astype(BF16), (mod1, norm_g[1:2]),
                           seq=seq, tm=t["tm_conv_out"], name="conv_out")
    qkvg = _attn_in(h1, attn_in_w[0].astype(BF16), bsz=bsz, seq=seq, dh=dh, tm=t["tm_attn_in"], tn=t["tn"])
    o = _stick_attn(qkvg, n_heads=N_HEADS, tk=MXU_DIM, n_sub=t["n_sub"])
    out = _out_residual(o.reshape(bsz * seq, d), x1, mod1, attn_out_w[0].astype(BF16),
                        (final_g.reshape(1, d),), seq=seq, tm=t["tm_attn_out"], name="attn_out")
    return out.reshape(bsz, seq, d)
```

```python
import functools
import math

import jax
import jax.numpy as jnp
from jax import lax
from jax.experimental import pallas as pl
from jax.experimental.pallas import tpu as pltpu

N_HEADS = 8
CONV_WIDTH = 3
RMS_EPS = 1e-6

SUBLANES = 8
MXU_DIM = 256
MIB = 1024 * 1024
VMEM_LIMIT = 56 * MIB
LAST_DOT_PIECES = 4

EXP2_ZERO_ABOVE = 151.0
LOG2E = 1.4426950408889634

F32 = jnp.float32
BF16 = jnp.bfloat16


def _silu(x):
    return x / (1.0 + jnp.exp(-x))


def _dot(a, b):
    return jnp.dot(a, b, preferred_element_type=F32)


def _norm_modulate(x, norm_g, shift, scale):
    ms = jnp.mean(x * x, axis=-1, keepdims=True)
    return (x * lax.rsqrt(ms + RMS_EPS) * norm_g) * (1.0 + scale) + shift


def _params(*semantics):
    return pltpu.CompilerParams(dimension_semantics=semantics, vmem_limit_bytes=VMEM_LIMIT)


def _adaln_kernel(c_ref, w_ref, b_ref, o_ref):
    c_act = _silu(c_ref[...])
    o_ref[0] = jnp.dot(c_act, w_ref[0], preferred_element_type=F32,
                       precision=lax.Precision.HIGHEST) + b_ref[0]


def _adaln_mod(c, ada_w, ada_b):
    depth, d, n3 = ada_w.shape
    bsz = c.shape[0]
    tn = 768
    return pl.pallas_call(
        _adaln_kernel,
        grid=(depth, n3 // tn),
        in_specs=[
            pl.BlockSpec((bsz, d), lambda l, n: (0, 0)),
            pl.BlockSpec((1, d, tn), lambda l, n: (l, 0, n)),
            pl.BlockSpec((1, 1, tn), lambda l, n: (l, 0, n)),
        ],
        out_specs=pl.BlockSpec((1, bsz, tn), lambda l, n: (l, 0, n)),
        out_shape=jax.ShapeDtypeStruct((depth, bsz, n3), F32),
        compiler_params=_params("arbitrary", "arbitrary"),
        name="adaln_mod",
    )(c, ada_w, ada_b.reshape(depth, 1, n3))


def _norm_rows_kernel(x_ref, mod_ref, ng_ref, o_ref):
    o_ref[...] = _norm_modulate(x_ref[...], ng_ref[...], mod_ref[0, 0:1, :], mod_ref[0, 1:2, :]).astype(BF16)


def _norm_first_tile(x2, mod, norm_g, *, tm):
    d = x2.shape[1]
    return pl.pallas_call(
        _norm_rows_kernel,
        grid=(1,),
        in_specs=[
            pl.BlockSpec((tm, d), lambda i: (0, 0)),
            pl.BlockSpec((1, 3, d), lambda i: (0, 0, 0)),
            pl.BlockSpec((1, d), lambda i: (0, 0)),
        ],
        out_specs=pl.BlockSpec((tm, d), lambda i: (0, 0)),
        out_shape=jax.ShapeDtypeStruct((tm, d), BF16),
        compiler_params=_params("arbitrary"),
        name="norm_first",
    )(x2, mod, norm_g)


def _conv_in_kernel(xn_ref, modn_ref, ng_ref, hfirst_ref, wb_ref, wc_ref, wu_ref, wg_ref, cw_ref, cb_ref,
                    y_ref, ha_ref, hb_ref, halo_ref, *, tiles_per_batch, chunk):
    i = pl.program_id(0)
    j = pl.program_id(1)
    tm = y_ref.shape[0]

    @pl.when(jnp.logical_and(i == 0, j == 0))
    def _():
        ha_ref[...] = hfirst_ref[...]

    def step(h_ref, hnext_ref):
        r0 = pl.multiple_of(j * chunk, chunk)
        hn = _norm_modulate(xn_ref[pl.ds(r0, chunk), :], ng_ref[...], modn_ref[0, 0:1, :], modn_ref[0, 1:2, :])
        hnext_ref[pl.ds(r0, chunk), :] = hn.astype(BF16)

        h = h_ref[...]
        v = _dot(h, wc_ref[...]) * _dot(h, wu_ref[...])
        first = (i % tiles_per_batch) == 0
        halo = jnp.where(first, 0.0, halo_ref[j])
        halo_ref[j] = v[tm - SUBLANES:, :]
        vfull = jnp.concatenate([halo, v], axis=0)
        vm1 = pltpu.roll(vfull, 1, 0)[SUBLANES:, :]
        vm2 = pltpu.roll(vfull, 2, 0)[SUBLANES:, :]
        conv = cb_ref[...] + cw_ref[0:1, :] * vm2 + cw_ref[1:2, :] * vm1 + cw_ref[2:3, :] * v
        gated = conv * _silu(_dot(h, wg_ref[...]))
        piece = tm // LAST_DOT_PIECES
        for r in range(LAST_DOT_PIECES):
            rows = slice(r * piece, (r + 1) * piece)
            y_ref[rows, :] = (_dot(h_ref[rows, :], wb_ref[...]) * gated[rows, :]).astype(BF16)

    pl.when(i % 2 == 0)(functools.partial(step, ha_ref, hb_ref))
    pl.when(i % 2 == 1)(functools.partial(step, hb_ref, ha_ref))


def _conv_in(x2, mod, norm_g, w_in, conv_w, conv_b, *, seq, tm, cn):
    rows, d = x2.shape
    nb = d // cn
    n_tiles = rows // tm
    tiles_per_batch = seq // tm
    h_first = _norm_first_tile(x2, mod, norm_g, tm=tm)
    kern = functools.partial(_conv_in_kernel, tiles_per_batch=tiles_per_batch, chunk=tm // nb)
    w_spec = lambda part: pl.BlockSpec((d, cn), lambda i, j, part=part: (0, part * nb + j))
    next_tile = lambda i: jnp.minimum(i + 1, n_tiles - 1)
    return pl.pallas_call(
        kern,
        grid=(n_tiles, nb),
        in_specs=[
            pl.BlockSpec((tm, d), lambda i, j: (next_tile(i), 0)),
            pl.BlockSpec((1, 3, d), lambda i, j: (next_tile(i) // tiles_per_batch, 0, 0)),
            pl.BlockSpec((1, d), lambda i, j: (0, 0)),
            pl.BlockSpec((tm, d), lambda i, j: (0, 0), pipeline_mode=pl.Buffered(1)),
            w_spec(0), w_spec(1), w_spec(2), w_spec(3),
            pl.BlockSpec((CONV_WIDTH, cn), lambda i, j: (0, j)),
            pl.BlockSpec((1, cn), lambda i, j: (0, j)),
        ],
        out_specs=pl.BlockSpec((tm, cn), lambda i, j: (i, j)),
        out_shape=jax.ShapeDtypeStruct((rows, d), BF16),
        scratch_shapes=[
            pltpu.VMEM((tm, d), BF16),
            pltpu.VMEM((tm, d), BF16),
            pltpu.VMEM((nb, SUBLANES, cn), F32),
        ],
        compiler_params=_params("arbitrary", "arbitrary"),
        name="conv_in",
    )(x2, mod, norm_g, h_first, w_in, w_in, w_in, w_in, conv_w, conv_b)


def _out_hidden_kernel(y_ref, x_ref, mod_ref, w_ref, nmod_ref, ng_ref, x1_ref, h_ref):
    r = x_ref[...] + mod_ref[0, 2:3, :] * _dot(y_ref[...], w_ref[...])
    x1_ref[...] = r
    h_ref[...] = _norm_modulate(r, ng_ref[...], nmod_ref[0, 0:1, :], nmod_ref[0, 1:2, :]).astype(BF16)


def _out_final_kernel(y_ref, x_ref, mod_ref, w_ref, fg_ref, o_ref):
    r = x_ref[...] + mod_ref[0, 2:3, :] * _dot(y_ref[...], w_ref[...])
    ms = jnp.mean(r * r, axis=-1, keepdims=True)
    o_ref[...] = r * lax.rsqrt(ms + RMS_EPS) * fg_ref[...]


def _out_residual(y2, x2, mod, w, tail, *, seq, tm, name):
    rows, d = x2.shape
    tiles_per_batch = seq // tm
    row_spec = pl.BlockSpec((tm, d), lambda i: (i, 0))
    mod_spec = pl.BlockSpec((1, 3, d), lambda i: (i // tiles_per_batch, 0, 0))
    vec_spec = pl.BlockSpec((1, d), lambda i: (0, 0))
    w_spec = pl.BlockSpec((d, d), lambda i: (0, 0), pipeline_mode=pl.Buffered(1))
    if len(tail) == 2:
        kern, tail_specs = _out_hidden_kernel, [mod_spec, vec_spec]
        out_specs = [row_spec, row_spec]
        out_shape = [jax.ShapeDtypeStruct((rows, d), F32), jax.ShapeDtypeStruct((rows, d), BF16)]
    else:
        kern, tail_specs = _out_final_kernel, [vec_spec]
        out_specs = row_spec
        out_shape = jax.ShapeDtypeStruct((rows, d), F32)
    return pl.pallas_call(
        kern,
        grid=(rows // tm,),
        in_specs=[row_spec, row_spec, mod_spec, w_spec] + tail_specs,
        out_specs=out_specs,
        out_shape=out_shape,
        compiler_params=_params("arbitrary"),
        name=name,
    )(y2, x2, mod, w, *tail)


def _attn_in_kernel(h_ref, w_ref, o_ref, *, q_blocks, q_scale):
    n = pl.program_id(1)
    p = _dot(h_ref[...], w_ref[...])
    p = p * jnp.where(n < q_blocks, q_scale, 1.0)
    dh = o_ref.shape[3]
    for hh in range(o_ref.shape[1]):
        o_ref[0, hh] = p[:, hh * dh:(hh + 1) * dh].astype(BF16)


def _attn_in(h2, w_in, *, bsz, seq, dh, tm, tn):
    rows, d = h2.shape
    n4 = w_in.shape[1]
    tiles_per_batch = seq // tm
    kern = functools.partial(_attn_in_kernel, q_blocks=d // tn, q_scale=LOG2E / math.sqrt(dh))
    return pl.pallas_call(
        kern,
        grid=(rows // tm, n4 // tn),
        in_specs=[
            pl.BlockSpec((tm, d), lambda i, n: (i, 0)),
            pl.BlockSpec((d, tn), lambda i, n: (0, n)),
        ],
        out_specs=pl.BlockSpec(
            (1, tn // dh, tm, dh),
            lambda i, n: (i // tiles_per_batch, n, i % tiles_per_batch, 0)),
        out_shape=jax.ShapeDtypeStruct((bsz, n4 // dh, seq, dh), BF16),
        compiler_params=_params("arbitrary", "arbitrary"),
        name="attn_in",
    )(h2, w_in)


def _stick_attn_kernel(q_ref, k_ref, v_ref, g_ref, u_ref, o_ref, acc_ref, run_ref, *, tk, n_sub):
    qi = pl.program_id(2)
    row = lax.broadcasted_iota(jnp.int32, (tk, tk), 0)
    col = lax.broadcasted_iota(jnp.int32, (tk, tk), 1)
    below_diag = col < row

    def scores(q, kb):
        k0 = pl.multiple_of(kb * tk, tk)
        z2 = lax.dot_general(q, k_ref[0, 0, pl.ds(k0, tk), :], (((1,), (1,)), ((), ())),
                             preferred_element_type=F32)
        nl = jnp.maximum(z2, 0.0) + jnp.log2(1.0 + jnp.exp2(-jnp.abs(z2)))
        return z2, nl

    def suffix_sum(nl):
        hi = nl.astype(BF16)
        lo = (nl - hi.astype(F32)).astype(BF16)
        return _dot(jnp.concatenate([hi, lo], axis=1), u_ref[...])

    def weighted_values(a, kb):
        k0 = pl.multiple_of(kb * tk, tk)
        return _dot(a.astype(BF16), v_ref[0, 0, pl.ds(k0, tk), :])

    run_mins = []
    for s in range(n_sub):
        rows = slice(s * tk, (s + 1) * tk)
        q = q_ref[0, 0, rows, :]
        kb_diag = qi * n_sub + s

        z2, nl = scores(q, kb_diag)
        nl = jnp.where(below_diag, nl, 0.0)
        cum = suffix_sum(nl)
        a = jnp.where(below_diag, jnp.exp2(z2 - cum), 0.0)
        acc = weighted_values(a, kb_diag)
        run = cum[:, 0:1]

        kb_prev = kb_diag - 1
        z2, nl = scores(q, jnp.maximum(kb_prev, 0))
        if s == 0:
            nl = jnp.where(qi > 0, nl, 0.0)
        cum = suffix_sum(nl)
        a = jnp.exp2(z2 - cum - run)
        if s == 0:
            a = jnp.where(qi > 0, a, 0.0)
        acc = acc + weighted_values(a, jnp.maximum(kb_prev, 0))
        run = run + cum[:, 0:1]
        acc_ref[s] = acc
        run_ref[s] = run
        run_mins.append(jnp.min(run))
        o_ref[0, rows, :] = (acc * _silu(g_ref[0, 0, rows, :].astype(F32))).astype(BF16)

    def more(carry):
        kb, run_min = carry
        return jnp.logical_and(kb >= 0, run_min < EXP2_ZERO_ABOVE)

    for s in range(n_sub):
        start = (qi * n_sub + s - 2, run_mins[s])

        @pl.when(more(start))
        def _(s=s, start=start):
            rows = slice(s * tk, (s + 1) * tk)
            q = q_ref[0, 0, rows, :]

            def visit(carry):
                kb, _ = carry
                z2, nl = scores(q, kb)
                cum = suffix_sum(nl)
                run = run_ref[s]
                acc_ref[s] += weighted_values(jnp.exp2(z2 - cum - run), kb)
                run = run + cum[:, 0:1]
                run_ref[s] = run
                return kb - 1, jnp.min(run)

            lax.while_loop(more, visit, start)
            o_ref[0, rows, :] = (acc_ref[s] * _silu(g_ref[0, 0, rows, :].astype(F32))).astype(BF16)


def _stick_attn(qkvg, *, n_heads, tk, n_sub):
    bsz, _, seq, dh = qkvg.shape
    tq = tk * n_sub
    tri = (lax.broadcasted_iota(jnp.int32, (tk, tk), 0)
           >= lax.broadcasted_iota(jnp.int32, (tk, tk), 1)).astype(BF16)
    tri = jnp.concatenate([tri, tri], axis=0)
    kern = functools.partial(_stick_attn_kernel, tk=tk, n_sub=n_sub)
    return pl.pallas_call(
        kern,
        grid=(bsz, n_heads, seq // tq),
        in_specs=[
            pl.BlockSpec((1, 1, tq, dh), lambda b, h, qi: (b, h, qi, 0)),
            pl.BlockSpec((1, 1, seq, dh), lambda b, h, qi: (b, n_heads + h, 0, 0)),
            pl.BlockSpec((1, 1, seq, dh), lambda b, h, qi: (b, 2 * n_heads + h, 0, 0)),
            pl.BlockSpec((1, 1, tq, dh), lambda b, h, qi: (b, 3 * n_heads + h, qi, 0)),
            pl.BlockSpec((2 * tk, tk), lambda b, h, qi: (0, 0), pipeline_mode=pl.Buffered(1)),
        ],
        out_specs=pl.BlockSpec((1, tq, dh), lambda b, h, qi: (b, qi, h)),
        out_shape=jax.ShapeDtypeStruct((bsz, seq, n_heads * dh), BF16),
        scratch_shapes=[
            pltpu.VMEM((n_sub, tk, dh), F32),
            pltpu.VMEM((n_sub, tk, 1), F32),
        ],
        compiler_params=_params("arbitrary", "arbitrary", "arbitrary"),
        name="stick_attn",
    )(qkvg, qkvg, qkvg, qkvg, tri)


def _tiles(seq):
    return dict(
        tm_conv_in=min(1024, seq),
        cn=2 * MXU_DIM,
        tm_conv_out=min(512, seq),
        tm_attn_in=min(1024, seq),
        tn=2048,
        tm_attn_out=min(512, seq),
        n_sub=min(16, seq // MXU_DIM),
    )


def kernel(x, c, norm_g, ada_w, ada_b, conv_in_w, conv_w, conv_b, conv_out_w, attn_in_w, attn_out_w, final_g):
    bsz, seq, d = x.shape
    dh = d // N_HEADS
    t = _tiles(seq)
    x2 = x.reshape(bsz * seq, d)
    mod = _adaln_mod(c, ada_w, ada_b)
    mod0 = mod[0].reshape(bsz, 3, d)
    mod1 = mod[1].reshape(bsz, 3, d)

    y0 = _conv_in(x2, mod0, norm_g[0:1], conv_in_w[0].astype(BF16), conv_w[0], conv_b[0:1],
                  seq=seq, tm=t["tm_conv_in"], cn=t["cn"])
    x1, h1 = _out_residual(y0, x2, mod0, conv_out_w[0].astype(BF16), (mod1, norm_g[1:2]),
                           seq=seq, tm=t["tm_conv_out"], name="conv_out")
    qkvg = _attn_in(h1, attn_in_w[0].astype(BF16), bsz=bsz, seq=seq, dh=dh, tm=t["tm_attn_in"], tn=t["tn"])
    o = _stick_attn(qkvg, n_heads=N_HEADS, tk=MXU_DIM, n_sub=t["n_sub"])
    out = _out_residual(o.reshape(bsz * seq, d), x1, mod1, attn_out_w[0].astype(BF16),
                        (final_g.reshape(1, d),), seq=seq, tm=t["tm_attn_out"], name="attn_out")
    return out.reshape(bsz, seq, d)
```

```python
import functools
import math

import jax
import jax.numpy as jnp
from jax import lax
from jax.experimental import pallas as pl
from jax.experimental.pallas import tpu as pltpu

N_HEADS = 8
CONV_WIDTH = 3
RMS_EPS = 1e-6

SUBLANES = 8
MXU_DIM = 256
MIB = 1024 * 1024
VMEM_LIMIT = 60 * MIB
LAST_DOT_PIECES = 4

EXP2_ZERO_ABOVE = 151.0
LOG2E = 1.4426950408889634

F32 = jnp.float32
BF16 = jnp.bfloat16


def _silu(x):
    return x / (1.0 + jnp.exp(-x))


def _dot(a, b):
    return jnp.dot(a, b, preferred_element_type=F32)


def _norm_modulate(x, norm_g, shift, scale):
    ms = jnp.mean(x * x, axis=-1, keepdims=True)
    return (x * lax.rsqrt(ms + RMS_EPS) * norm_g) * (1.0 + scale) + shift


def _params(*semantics):
    return pltpu.CompilerParams(dimension_semantics=semantics, vmem_limit_bytes=VMEM_LIMIT)


def _adaln_kernel(ct_ref, w_ref, b_ref, o_ref):
    c_act = _silu(ct_ref[...])
    w = w_ref[0]
    for b in range(ct_ref.shape[1]):
        o_ref[0, b:b + 1, :] = jnp.sum(c_act[:, b:b + 1] * w, axis=0, keepdims=True) + b_ref[0]


def _adaln_mod(c, ada_w, ada_b):
    depth, d, n3 = ada_w.shape
    bsz = c.shape[0]
    tn = 768
    return pl.pallas_call(
        _adaln_kernel,
        grid=(depth, n3 // tn),
        in_specs=[
            pl.BlockSpec((d, bsz), lambda l, n: (0, 0)),
            pl.BlockSpec((1, d, tn), lambda l, n: (l, 0, n)),
            pl.BlockSpec((1, 1, tn), lambda l, n: (l, 0, n)),
        ],
        out_specs=pl.BlockSpec((1, bsz, tn), lambda l, n: (l, 0, n)),
        out_shape=jax.ShapeDtypeStruct((depth, bsz, n3), F32),
        compiler_params=_params("arbitrary", "arbitrary"),
        name="adaln_mod",
    )(c.T, ada_w, ada_b.reshape(depth, 1, n3))


def _norm_rows_kernel(x_ref, mod_ref, ng_ref, o_ref):
    o_ref[...] = _norm_modulate(x_ref[...], ng_ref[...], mod_ref[0, 0:1, :], mod_ref[0, 1:2, :]).astype(BF16)


def _norm_first_tile(x2, mod, norm_g, *, tm):
    d = x2.shape[1]
    return pl.pallas_call(
        _norm_rows_kernel,
        grid=(1,),
        in_specs=[
            pl.BlockSpec((tm, d), lambda i: (0, 0)),
            pl.BlockSpec((1, 3, d), lambda i: (0, 0, 0)),
            pl.BlockSpec((1, d), lambda i: (0, 0)),
        ],
        out_specs=pl.BlockSpec((tm, d), lambda i: (0, 0)),
        out_shape=jax.ShapeDtypeStruct((tm, d), BF16),
        compiler_params=_params("arbitrary"),
        name="norm_first",
    )(x2, mod, norm_g)


def _conv_in_kernel(xn_ref, modn_ref, ng_ref, hfirst_ref, wb_ref, wc_ref, wu_ref, wg_ref, cw_ref, cb_ref,
                    y_ref, ha_ref, hb_ref, halo_ref, *, tiles_per_batch, chunk):
    i = pl.program_id(0)
    j = pl.program_id(1)
    tm = y_ref.shape[0]

    @pl.when(jnp.logical_and(i == 0, j == 0))
    def _():
        ha_ref[...] = hfirst_ref[...]

    def step(h_ref, hnext_ref):
        r0 = pl.multiple_of(j * chunk, chunk)
        hn = _norm_modulate(xn_ref[pl.ds(r0, chunk), :], ng_ref[...], modn_ref[0, 0:1, :], modn_ref[0, 1:2, :])
        hnext_ref[pl.ds(r0, chunk), :] = hn.astype(BF16)

        h = h_ref[...]
        v = _dot(h, wc_ref[...]) * _dot(h, wu_ref[...])
        first = (i % tiles_per_batch) == 0
        halo = jnp.where(first, 0.0, halo_ref[j])
        halo_ref[j] = v[tm - SUBLANES:, :]
        vfull = jnp.concatenate([halo, v], axis=0)
        vm1 = pltpu.roll(vfull, 1, 0)[SUBLANES:, :]
        vm2 = pltpu.roll(vfull, 2, 0)[SUBLANES:, :]
        conv = cb_ref[...] + cw_ref[0:1, :] * vm2 + cw_ref[1:2, :] * vm1 + cw_ref[2:3, :] * v
        gated = conv * _silu(_dot(h, wg_ref[...]))
        piece = tm // LAST_DOT_PIECES
        for r in range(LAST_DOT_PIECES):
            rows = slice(r * piece, (r + 1) * piece)
            y_ref[rows, :] = (_dot(h_ref[rows, :], wb_ref[...]) * gated[rows, :]).astype(BF16)

    pl.when(i % 2 == 0)(functools.partial(step, ha_ref, hb_ref))
    pl.when(i % 2 == 1)(functools.partial(step, hb_ref, ha_ref))


def _conv_in(x2, mod, norm_g, w_in, conv_w, conv_b, *, seq, tm, cn):
    rows, d = x2.shape
    nb = d // cn
    n_tiles = rows // tm
    tiles_per_batch = seq // tm
    h_first = _norm_first_tile(x2, mod, norm_g, tm=tm)
    kern = functools.partial(_conv_in_kernel, tiles_per_batch=tiles_per_batch, chunk=tm // nb)
    w_spec = lambda part: pl.BlockSpec((d, cn), lambda i, j, part=part: (0, part * nb + j))
    next_tile = lambda i: jnp.minimum(i + 1, n_tiles - 1)
    return pl.pallas_call(
        kern,
        grid=(n_tiles, nb),
        in_specs=[
            pl.BlockSpec((tm, d), lambda i, j: (next_tile(i), 0)),
            pl.BlockSpec((1, 3, d), lambda i, j: (next_tile(i) // tiles_per_batch, 0, 0)),
            pl.BlockSpec((1, d), lambda i, j: (0, 0)),
            pl.BlockSpec((tm, d), lambda i, j: (0, 0), pipeline_mode=pl.Buffered(1)),
            w_spec(0), w_spec(1), w_spec(2), w_spec(3),
            pl.BlockSpec((CONV_WIDTH, cn), lambda i, j: (0, j)),
            pl.BlockSpec((1, cn), lambda i, j: (0, j)),
        ],
        out_specs=pl.BlockSpec((tm, cn), lambda i, j: (i, j)),
        out_shape=jax.ShapeDtypeStruct((rows, d), BF16),
        scratch_shapes=[
            pltpu.VMEM((tm, d), BF16),
            pltpu.VMEM((tm, d), BF16),
            pltpu.VMEM((nb, SUBLANES, cn), F32),
        ],
        compiler_params=_params("arbitrary", "arbitrary"),
        name="conv_in",
    )(x2, mod, norm_g, h_first, w_in, w_in, w_in, w_in, conv_w, conv_b)


def _out_hidden_kernel(y_ref, x_ref, mod_ref, w_ref, nmod_ref, ng_ref, x1_ref, h_ref):
    r = x_ref[...] + mod_ref[0, 2:3, :] * _dot(y_ref[...], w_ref[...])
    x1_ref[...] = r
    h_ref[...] = _norm_modulate(r, ng_ref[...], nmod_ref[0, 0:1, :], nmod_ref[0, 1:2, :]).astype(BF16)


def _out_final_kernel(y_ref, x_ref, mod_ref, w_ref, fg_ref, o_ref):
    r = x_ref[...] + mod_ref[0, 2:3, :] * _dot(y_ref[...], w_ref[...])
    ms = jnp.mean(r * r, axis=-1, keepdims=True)
    o_ref[...] = r * lax.rsqrt(ms + RMS_EPS) * fg_ref[...]


def _out_residual(y2, x2, mod, w, tail, *, seq, tm, name):
    rows, d = x2.shape
    tiles_per_batch = seq // tm
    row_spec = pl.BlockSpec((tm, d), lambda i: (i, 0))
    mod_spec = pl.BlockSpec((1, 3, d), lambda i: (i // tiles_per_batch, 0, 0))
    vec_spec = pl.BlockSpec((1, d), lambda i: (0, 0))
    w_spec = pl.BlockSpec((d, d), lambda i: (0, 0), pipeline_mode=pl.Buffered(1))
    if len(tail) == 2:
        kern, tail_specs = _out_hidden_kernel, [mod_spec, vec_spec]
        out_specs = [row_spec, row_spec]
        out_shape = [jax.ShapeDtypeStruct((rows, d), F32), jax.ShapeDtypeStruct((rows, d), BF16)]
    else:
        kern, tail_specs = _out_final_kernel, [vec_spec]
        out_specs = row_spec
        out_shape = jax.ShapeDtypeStruct((rows, d), F32)
    return pl.pallas_call(
        kern,
        grid=(rows // tm,),
        in_specs=[row_spec, row_spec, mod_spec, w_spec] + tail_specs,
        out_specs=out_specs,
        out_shape=out_shape,
        compiler_params=_params("arbitrary"),
        name=name,
    )(y2, x2, mod, w, *tail)


def _attn_in_kernel(h_ref, w_ref, o_ref, *, q_blocks, q_scale):
    n = pl.program_id(1)
    p = _dot(h_ref[...], w_ref[...])
    p = p * jnp.where(n < q_blocks, q_scale, 1.0)
    dh = o_ref.shape[3]
    for hh in range(o_ref.shape[1]):
        o_ref[0, hh] = p[:, hh * dh:(hh + 1) * dh].astype(BF16)


def _attn_in(h2, w_in, *, bsz, seq, dh, tm, tn):
    rows, d = h2.shape
    n4 = w_in.shape[1]
    tiles_per_batch = seq // tm
    kern = functools.partial(_attn_in_kernel, q_blocks=d // tn, q_scale=LOG2E / math.sqrt(dh))
    return pl.pallas_call(
        kern,
        grid=(rows // tm, n4 // tn),
        in_specs=[
            pl.BlockSpec((tm, d), lambda i, n: (i, 0)),
            pl.BlockSpec((d, tn), lambda i, n: (0, n)),
        ],
        out_specs=pl.BlockSpec(
            (1, tn // dh, tm, dh),
            lambda i, n: (i // tiles_per_batch, n, i % tiles_per_batch, 0)),
        out_shape=jax.ShapeDtypeStruct((bsz, n4 // dh, seq, dh), BF16),
        compiler_params=_params("arbitrary", "arbitrary"),
        name="attn_in",
    )(h2, w_in)


def _stick_attn_kernel(q_ref, k_ref, v_ref, g_ref, u_ref, o_ref, acc_ref, run_ref, *, tk, n_sub):
    qi = pl.program_id(2)
    row = lax.broadcasted_iota(jnp.int32, (tk, tk), 0)
    col = lax.broadcasted_iota(jnp.int32, (tk, tk), 1)
    below_diag = col < row

    def scores(q, kb):
        k0 = pl.multiple_of(kb * tk, tk)
        z2 = lax.dot_general(q, k_ref[0, 0, pl.ds(k0, tk), :], (((1,), (1,)), ((), ())),
                             preferred_element_type=F32)
        nl = jnp.maximum(z2, 0.0) + jnp.log2(1.0 + jnp.exp2(-jnp.abs(z2)))
        return z2, nl

    def suffix_sum(nl):
        hi = nl.astype(BF16)
        lo = (nl - hi.astype(F32)).astype(BF16)
        return _dot(jnp.concatenate([hi, lo], axis=1), u_ref[...])

    def weighted_values(a, kb):
        k0 = pl.multiple_of(kb * tk, tk)
        return _dot(a.astype(BF16), v_ref[0, 0, pl.ds(k0, tk), :])

    run_mins = []
    for s in range(n_sub):
        rows = slice(s * tk, (s + 1) * tk)
        q = q_ref[0, 0, rows, :]
        kb_diag = qi * n_sub + s

        z2, nl = scores(q, kb_diag)
        nl = jnp.where(below_diag, nl, 0.0)
        cum = suffix_sum(nl)
        a = jnp.where(below_diag, jnp.exp2(z2 - cum), 0.0)
        acc = weighted_values(a, kb_diag)
        run = cum[:, 0:1]

        kb_prev = kb_diag - 1
        z2, nl = scores(q, jnp.maximum(kb_prev, 0))
        if s == 0:
            nl = jnp.where(qi > 0, nl, 0.0)
        cum = suffix_sum(nl)
        a = jnp.exp2(z2 - cum - run)
        if s == 0:
            a = jnp.where(qi > 0, a, 0.0)
        acc = acc + weighted_values(a, jnp.maximum(kb_prev, 0))
        run = run + cum[:, 0:1]
        acc_ref[s] = acc
        run_ref[s] = run
        run_mins.append(jnp.min(run))
        o_ref[0, rows, :] = (acc * _silu(g_ref[0, 0, rows, :].astype(F32))).astype(BF16)

    def more(carry):
        kb, run_min = carry
        return jnp.logical_and(kb >= 0, run_min < EXP2_ZERO_ABOVE)

    for s in range(n_sub):
        start = (qi * n_sub + s - 2, run_mins[s])

        @pl.when(more(start))
        def _(s=s, start=start):
            rows = slice(s * tk, (s + 1) * tk)
            q = q_ref[0, 0, rows, :]

            def visit(carry):
                kb, _ = carry
                z2, nl = scores(q, kb)
                cum = suffix_sum(nl)
                run = run_ref[s]
                acc_ref[s] += weighted_values(jnp.exp2(z2 - cum - run), kb)
                run = run + cum[:, 0:1]
                run_ref[s] = run
                return kb - 1, jnp.min(run)

            lax.while_loop(more, visit, start)
            o_ref[0, rows, :] = (acc_ref[s] * _silu(g_ref[0, 0, rows, :].astype(F32))).astype(BF16)


def _stick_attn(qkvg, *, n_heads, tk, n_sub):
    bsz, _, seq, dh = qkvg.shape
    tq = tk * n_sub
    tri = (lax.broadcasted_iota(jnp.int32, (tk, tk), 0)
           >= lax.broadcasted_iota(jnp.int32, (tk, tk), 1)).astype(BF16)
    tri = jnp.concatenate([tri, tri], axis=0)
    kern = functools.partial(_stick_attn_kernel, tk=tk, n_sub=n_sub)
    return pl.pallas_call(
        kern,
        grid=(bsz, n_heads, seq // tq),
        in_specs=[
            pl.BlockSpec((1, 1, tq, dh), lambda b, h, qi: (b, h, qi, 0)),
            pl.BlockSpec((1, 1, seq, dh), lambda b, h, qi: (b, n_heads + h, 0, 0)),
            pl.BlockSpec((1, 1, seq, dh), lambda b, h, qi: (b, 2 * n_heads + h, 0, 0)),
            pl.BlockSpec((1, 1, tq, dh), lambda b, h, qi: (b, 3 * n_heads + h, qi, 0)),
            pl.BlockSpec((2 * tk, tk), lambda b, h, qi: (0, 0), pipeline_mode=pl.Buffered(1)),
        ],
        out_specs=pl.BlockSpec((1, tq, dh), lambda b, h, qi: (b, qi, h)),
        out_shape=jax.ShapeDtypeStruct((bsz, seq, n_heads * dh), BF16),
        scratch_shapes=[
            pltpu.VMEM((n_sub, tk, dh), F32),
            pltpu.VMEM((n_sub, tk, 1), F32),
        ],
        compiler_params=_params("arbitrary", "arbitrary", "arbitrary"),
        name="stick_attn",
    )(qkvg, qkvg, qkvg, qkvg, tri)


def _tiles(seq):
    return dict(
        tm_conv_in=min(1024, seq),
        cn=2 * MXU_DIM,
        tm_conv_out=min(512, seq),
        tm_attn_in=min(1024, seq),
        tn=2048,
        tm_attn_out=min(1024, seq),
        n_sub=min(16, seq // MXU_DIM),
    )


def kernel(x, c, norm_g, ada_w, ada_b, conv_in_w, conv_w, conv_b, conv_out_w, attn_in_w, attn_out_w, final_g):
    bsz, seq, d = x.shape
    dh = d // N_HEADS
    t = _tiles(seq)
    x2 = x.reshape(bsz * seq, d)
    mod = _adaln_mod(c, ada_w, ada_b)
    mod0 = mod[0].reshape(bsz, 3, d)
    mod1 = mod[1].reshape(bsz, 3, d)

    y0 = _conv_in(x2, mod0, norm_g[0:1], conv_in_w[0].astype(BF16), conv_w[0], conv_b[0:1],
                  seq=seq, tm=t["tm_conv_in"], cn=t["cn"])
    x1, h1 = _out_residual(y0, x2, mod0, conv_out_w[0].astype(BF16), (mod1, norm_g[1:2]),
                           seq=seq, tm=t["tm_conv_out"], name="conv_out")
    qkvg = _attn_in(h1, attn_in_w[0].astype(BF16), bsz=bsz, seq=seq, dh=dh, tm=t["tm_attn_in"], tn=t["tn"])
    o = _stick_attn(qkvg, n_heads=N_HEADS, tk=MXU_DIM, n_sub=t["n_sub"])
    out = _out_residual(o.reshape(bsz * seq, d), x1, mod1, attn_out_w[0].astype(BF16),
                        (final_g.reshape(1, d),), seq=seq, tm=t["tm_attn_out"], name="attn_out")
    return out.reshape(bsz, seq, d)
```
